```python
import math
import jax
import jax.numpy as jnp
from jax import lax
import numpy as np

D_MODEL = 1024
BATCH = 8
SEQ = 4096
DEPTH = 2

MEM_LEN = 256
MIX_W = 512
N_BRANCH = 3
CONV_K = 3
SB_HEADS = 8
SB_HEAD_DIM = MIX_W // SB_HEADS
SB_BLOCK = 128
RWKV_HEADS = 8
RWKV_HEAD = MIX_W // RWKV_HEADS
DECAY_LORA = 64
ICLR_LORA = 64
GATE_LORA = 128
VRES_LORA = 32
DECAY_SCALE = math.exp(-0.5)
CA_HEADS = 4
CA_HEAD_DIM = D_MODEL // CA_HEADS
FFN_DIM = 2816
FFN_CONV_K = 3
RMS_EPS = 1e-6
GN_EPS = 64e-5

CONV_COLS = 3 * MIX_W
SB_COLS = 3 * MIX_W
GATE_COLS = N_BRANCH * D_MODEL
RWKV_COLS = 3 * MIX_W + DECAY_LORA + ICLR_LORA + GATE_LORA
COMB_COLS = CONV_COLS + SB_COLS + GATE_COLS + RWKV_COLS
OFF_SB = CONV_COLS
OFF_GATE = OFF_SB + SB_COLS
OFF_RWKV = OFF_GATE + GATE_COLS

kernel_name = "hybrid_gated_conv_stickbreak_rwkv7_block"


def rms_norm(x, g):
    xf = x.astype(jnp.float32)
    y = xf * lax.rsqrt(jnp.mean(xf * xf, axis=-1, keepdims=True) + RMS_EPS)
    return (y * g.astype(jnp.float32)).astype(x.dtype)


def causal_dwconv(x, w):
    c = x.shape[-1]
    return lax.conv_general_dilated(
        x, w[:, None, :].astype(x.dtype), window_strides=(1,),
        padding=[(w.shape[0] - 1, 0)], dimension_numbers=("NWC", "WIO", "NWC"),
        feature_group_count=c)


def stick_breaking_attention(q, k, v):
    s_len, dh = q.shape[2], q.shape[3]
    scale = dh ** -0.5
    outs = []
    for i in range(s_len // SB_BLOCK):
        start, end = i * SB_BLOCK, (i + 1) * SB_BLOCK
        q_blk = q[:, :, start:end]
        k_blk, v_blk = k[:, :, :end], v[:, :, :end]
        z = jnp.einsum("bhqd,bhkd->bhqk", q_blk, k_blk).astype(jnp.float32) * scale
        q_pos = start + jnp.arange(SB_BLOCK)
        k_pos = jnp.arange(end)
        mask = k_pos[None, :] < q_pos[:, None]
        log_not = jnp.where(mask, jax.nn.log_sigmoid(-z), 0.0)
        suffix = lax.cumsum(log_not, axis=3, reverse=True) - log_not
        log_w = jax.nn.log_sigmoid(z) + suffix
        att = jnp.where(mask, jnp.exp(log_w), 0.0)
        outs.append(jnp.einsum("bhqk,bhkd->bhqd", att.astype(v.dtype), v_blk))
    return jnp.concatenate(outs, axis=2)


def rwkv7_time_mix(p, mu, w0, w_lora, a0, a_lora, g_lora, k_k, k_a, r_k, lnx_w, lnx_b,
                   v_first, v0, v_lora):
    b, s, _ = p.shape
    f32 = jnp.float32
    p = p.astype(f32)
    prev = jnp.pad(p, ((0, 0), (1, 0), (0, 0)))[:, :-1]
    p = p + mu.astype(f32) * (prev - p)
    w_ = MIX_W
    r, k, v = p[..., :w_], p[..., w_:2 * w_], p[..., 2 * w_:3 * w_]
    o = 3 * w_
    xw = p[..., o:o + DECAY_LORA]; o += DECAY_LORA
    xa = p[..., o:o + ICLR_LORA]; o += ICLR_LORA
    xg = p[..., o:o + GATE_LORA]; o += GATE_LORA
    log_decay = -DECAY_SCALE * jax.nn.sigmoid(w0 + jnp.tanh(xw) @ w_lora)
    a = jax.nn.sigmoid(a0 + xa @ a_lora)
    if v0 is None:
        v_first = v
    else:
        xv = p[..., o:o + VRES_LORA]
        v = v + (v_first - v) * jax.nn.sigmoid(v0 + xv @ v_lora)
    g = jax.nn.sigmoid(xg) @ g_lora

    def heads(t):
        return t.reshape(b, s, RWKV_HEADS, RWKV_HEAD)

    kk = heads(k * k_k)
    kk = kk / jnp.maximum(jnp.linalg.norm(kk, axis=-1, keepdims=True), 1e-12)
    k = k * (1.0 + (a - 1.0) * k_a)
    rh, kh, vh, ah, wh = heads(r), heads(k), heads(v), heads(a), heads(jnp.exp(log_decay))

    def step(state, inp):
        r_t, w_t, k_t, v_t, kk_t, a_t = inp
        sa = jnp.einsum("bhvk,bhk->bhv", state, kk_t)
        state = (state * w_t[:, :, None, :]
                 - sa[..., None] * (kk_t * a_t)[:, :, None, :]
                 + v_t[..., None] * k_t[:, :, None, :])
        return state, jnp.einsum("bhvk,bhk->bhv", state, r_t)

    xs = tuple(jnp.moveaxis(t, 1, 0) for t in (rh, wh, kh, vh, kk, ah))
    state0 = jnp.zeros((b, RWKV_HEADS, RWKV_HEAD, RWKV_HEAD), f32)
    _, out = lax.scan(step, state0, xs)
    out = jnp.moveaxis(out, 0, 1)
    mean = jnp.mean(out, axis=-1, keepdims=True)
    var = jnp.mean(jnp.square(out - mean), axis=-1, keepdims=True)
    out = ((out - mean) * lax.rsqrt(var + GN_EPS)).reshape(b, s, w_) * lnx_w + lnx_b
    bonus = (jnp.sum(rh * kh * r_k, axis=-1, keepdims=True) * vh).reshape(b, s, w_)
    return (out + bonus) * g, v_first


def hybrid_token_mixer(xn, w_comb, conv_w, mu_rwkv, w0, w_lora, a0, a_lora, g_lora, k_k, k_a,
                       r_k, lnx_w, lnx_b, branch_proj, gate_b, w_mix_out, v_first, vres):
    b, s, _ = xn.shape
    mu = mu_rwkv
    v0 = v_lora = None
    if vres is not None:
        w_vres, mu_vres, v0, v_lora = vres
        w_comb = jnp.concatenate([w_comb, w_vres], axis=1)
        mu = jnp.concatenate([mu_rwkv, mu_vres], axis=0)
    proj = xn @ w_comb

    c_b, c_c, c_x = jnp.split(proj[..., :OFF_SB], 3, axis=-1)
    y_conv = c_b * causal_dwconv(c_c * c_x, conv_w)

    q, k, v = jnp.split(proj[..., OFF_SB:OFF_GATE], 3, axis=-1)

    def to_heads(t):
        return t.reshape(b, s, SB_HEADS, SB_HEAD_DIM).transpose(0, 2, 1, 3)

    y_sb = stick_breaking_attention(to_heads(q), to_heads(k), to_heads(v))
    y_sb = y_sb.transpose(0, 2, 1, 3).reshape(b, s, MIX_W)

    y_rwkv, v_first = rwkv7_time_mix(proj[..., OFF_RWKV:], mu, w0, w_lora, a0, a_lora, g_lora,
                                     k_k, k_a, r_k, lnx_w, lnx_b, v_first, v0, v_lora)

    gates = jax.nn.sigmoid(
        proj[..., OFF_GATE:OFF_RWKV].reshape(b, s, N_BRANCH, D_MODEL).astype(jnp.float32)
        + gate_b).astype(xn.dtype)
    ys = jnp.stack([y_conv.astype(xn.dtype), y_sb.astype(xn.dtype), y_rwkv.astype(xn.dtype)],
                   axis=2)
    branches = jnp.einsum("bsnc,ncd->bsnd", ys, branch_proj)
    merged = jnp.einsum("bsnd,bsnd->bsd", gates, branches)
    return merged @ w_mix_out, v_first


def memory_cross_attention(hn, mem_n, wq, wkv, wo):
    b, s, _ = hn.shape
    m = mem_n.shape[1]
    q = (hn @ wq).reshape(b, s, CA_HEADS, CA_HEAD_DIM)
    k, v = jnp.split(mem_n @ wkv, 2, axis=-1)
    k = k.reshape(b, m, CA_HEADS, CA_HEAD_DIM)
    v = v.reshape(b, m, CA_HEADS, CA_HEAD_DIM)
    scores = jnp.einsum("bshd,bmhd->bhsm", q, k).astype(jnp.float32) * (CA_HEAD_DIM ** -0.5)
    probs = jax.nn.softmax(scores, axis=-1).astype(v.dtype)
    out = jnp.einsum("bhsm,bmhd->bshd", probs, v).reshape(b, s, D_MODEL)
    return out @ wo


def conv_ffn(hn, w_up, conv_w, conv_b, w_down):
    u = causal_dwconv(hn @ w_up, conv_w) + conv_b
    gate, val = jnp.split(u, 2, axis=-1)
    return (jax.nn.silu(gate) * val) @ w_down


def setup_inputs(seed: int = 0) -> dict:
    key = jax.random.key(seed)
    ks = iter(jax.random.split(key, 48))

    def nrm(shape, scale):
        return jax.random.normal(next(ks), shape, jnp.float32) * scale

    def gain(shape):
        return 1.0 + nrm(shape, 0.02)

    L, Lv, D, W, F = DEPTH, DEPTH - 1, D_MODEL, MIX_W, FFN_DIM
    return {
        "x": nrm((BATCH, SEQ, D), 1.0),
        "mem": nrm((BATCH, MEM_LEN, D), 1.0),
        "norm_mix": gain((L, D)),
        "w_comb": nrm((L, D, COMB_COLS), D ** -0.5),
        "conv_w": nrm((L, CONV_K, W), 0.5),
        "mu_rwkv": jax.random.uniform(next(ks), (L, RWKV_COLS), jnp.float32),
        "w0": 0.5 + nrm((L, W), 0.5),
        "w_lora": nrm((L, DECAY_LORA, W), 0.1),
        "a0": nrm((L, W), 0.3),
        "a_lora": nrm((L, ICLR_LORA, W), 0.1),
        "g_lora": nrm((L, GATE_LORA, W), GATE_LORA ** -0.5),
        "k_k": 1.0 + nrm((L, W), 0.1),
        "k_a": 1.0 + nrm((L, W), 0.1),
        "r_k": nrm((L, RWKV_HEADS, RWKV_HEAD), 0.1),
        "lnx_w": gain((L, W)),
        "lnx_b": nrm((L, W), 0.02),
        "w_vres": nrm((Lv, D, VRES_LORA), D ** -0.5),
        "mu_vres": jax.random.uniform(next(ks), (Lv, VRES_LORA), jnp.float32),
        "v0": nrm((Lv, W), 0.3),
        "v_lora": nrm((Lv, VRES_LORA, W), 0.1),
        "branch_proj": nrm((L, N_BRANCH, W, D), W ** -0.5),
        "gate_b": nrm((L, N_BRANCH, D), 0.01),
        "w_mix_out": nrm((L, D, D), D ** -0.5),
        "norm_ca": gain((L, D)),
        "norm_mem": gain((L, D)),
        "ca_wq": nrm((L, D, D), D ** -0.5),
        "ca_wkv": nrm((L, D, 2 * D), D ** -0.5),
        "ca_wo": nrm((L, D, D), D ** -0.5),
        "norm_ffn": gain((L, D)),
        "ffn_up": nrm((L, D, 2 * F), D ** -0.5),
        "ffn_conv_w": nrm((L, FFN_CONV_K, 2 * F), 0.5),
        "ffn_conv_b": nrm((L, 2 * F), 0.02),
        "ffn_down": nrm((L, F, D), F ** -0.5),
        "norm_final": gain((D,)),
    }


def reference(x, mem, norm_mix, w_comb, conv_w, mu_rwkv, w0, w_lora, a0, a_lora, g_lora, k_k,
              k_a, r_k, lnx_w, lnx_b, w_vres, mu_vres, v0, v_lora, branch_proj, gate_b,
              w_mix_out, norm_ca, norm_mem, ca_wq, ca_wkv, ca_wo, norm_ffn, ffn_up, ffn_conv_w,
              ffn_conv_b, ffn_down, norm_final):
    h = x
    v_first = None
    for l in range(DEPTH):
        vres = None if l == 0 else (w_vres[l - 1], mu_vres[l - 1], v0[l - 1], v_lora[l - 1])
        mix, v_first = hybrid_token_mixer(
            rms_norm(h, norm_mix[l]), w_comb[l], conv_w[l], mu_rwkv[l], w0[l], w_lora[l], a0[l],
            a_lora[l], g_lora[l], k_k[l], k_a[l], r_k[l], lnx_w[l], lnx_b[l], branch_proj[l],
            gate_b[l], w_mix_out[l], v_first, vres)
        h = h + mix
        h = h + memory_cross_attention(rms_norm(h, norm_ca[l]), rms_norm(mem, norm_mem[l]),
                                       ca_wq[l], ca_wkv[l], ca_wo[l])
        h = h + conv_ffn(rms_norm(h, norm_ffn[l]), ffn_up[l], ffn_conv_w[l], ffn_conv_b[l],
                         ffn_down[l])
    return rms_norm(h, norm_final)
```

```python
import functools
import math

import jax
import jax.numpy as jnp
from jax import lax
from jax.experimental import pallas as pl
from jax.experimental.pallas import tpu as pltpu

F32 = jnp.float32
BF16 = jnp.bfloat16

D_MODEL = 1024
MIX_W = 512
N_BRANCH = 3
SB_HEADS = 8
SB_HEAD_DIM = MIX_W // SB_HEADS
RWKV_HEADS = 8
RWKV_HEAD = MIX_W // RWKV_HEADS
DECAY_LORA = 64
ICLR_LORA = 64
GATE_LORA = 128
VRES_LORA = 32
DECAY_SCALE = math.exp(-0.5)
CA_HEADS = 4
CA_HEAD_DIM = D_MODEL // CA_HEADS
FFN_DIM = 2816
RMS_EPS = 1e-6
GN_EPS = 64e-5

MAIN_COLS = 3 * MIX_W + 3 * MIX_W + N_BRANCH * D_MODEL
RWKV_COLS = 3 * MIX_W + DECAY_LORA + ICLR_LORA + GATE_LORA
LANES = 128
RWKV_COLS_VRES = RWKV_COLS + LANES
VMEM_LIMIT = 48 * 1024 * 1024
FFN_VMEM_LIMIT = 56 * 1024 * 1024

SB_BLOCK = 128
SB_DEAD = -104.0
RWKV_CHUNK = 64
RWKV_PAIR = 2 * RWKV_HEAD


def _dot(a, b):
    return jnp.dot(a.astype(BF16), b.astype(BF16), preferred_element_type=F32)


def _dot_nt(a, b):
    return lax.dot_general(a.astype(BF16), b.astype(BF16), (((1,), (1,)), ((), ())),
                           preferred_element_type=F32)


def _dot_hilo(a, b):
    hi = a.astype(BF16)
    lo = (a - hi.astype(F32)).astype(BF16)
    return (jnp.dot(hi, b, preferred_element_type=F32)
            + jnp.dot(lo, b, preferred_element_type=F32))


def _rms(x, g):
    ms = jnp.mean(x * x, axis=-1, keepdims=True)
    return x * lax.rsqrt(ms + RMS_EPS) * g


def _shift_rows(u, prev8, n):
    rows = lax.broadcasted_iota(jnp.int32, u.shape, 0)
    r = pltpu.roll(u, n, 0)
    if n == 1:
        return jnp.where(rows == 0, prev8[7:8], r)
    return jnp.where(rows == 0, prev8[6:7], jnp.where(rows == 1, prev8[7:8], r))


def _pick(n, cands):
    for c in cands:
        if n % c == 0:
            return c
    raise ValueError(f"no tile for {n} in {cands}")


def _norm_mm_kernel(x_ref, g_ref, w_ref, o_ref, xn_ref):
    @pl.when(pl.program_id(1) == 0)
    def _():
        xn_ref[...] = _rms(x_ref[...], g_ref[...]).astype(BF16)

    o_ref[...] = jnp.dot(xn_ref[...], w_ref[...], preferred_element_type=F32).astype(o_ref.dtype)


def norm_matmul(x, gain, w, out_dtype):
    m, k = x.shape
    n = w.shape[1]
    tm = _pick(m, (1024, 512, 256, 128))
    tn = _pick(n, (512, 384, 256, 128))
    return pl.pallas_call(
        _norm_mm_kernel,
        grid=(m // tm, n // tn),
        in_specs=[pl.BlockSpec((tm, k), lambda i, j: (i, 0)),
                  pl.BlockSpec((1, k), lambda i, j: (0, 0)),
                  pl.BlockSpec((k, tn), lambda i, j: (0, j))],
        out_specs=pl.BlockSpec((tm, tn), lambda i, j: (i, j)),
        out_shape=jax.ShapeDtypeStruct((m, n), out_dtype),
        scratch_shapes=[pltpu.VMEM((tm, k), BF16)],
        compiler_params=pltpu.CompilerParams(
            dimension_semantics=("parallel", "arbitrary"), vmem_limit_bytes=VMEM_LIMIT),
        name="norm_matmul",
    )(x, gain.reshape(1, k), w)


def _conv_mixer_kernel(cb_ref, cc_ref, cx_ref, w_ref, o_ref, carry_ref):
    @pl.when(pl.program_id(1) == 0)
    def _():
        carry_ref[...] = jnp.zeros_like(carry_ref)

    u = cc_ref[...].astype(F32) * cx_ref[...].astype(F32)
    prev8 = carry_ref[...]
    y = w_ref[2:3] * u + w_ref[1:2] * _shift_rows(u, prev8, 1) + w_ref[0:1] * _shift_rows(u, prev8, 2)
    carry_ref[...] = u[u.shape[0] - 8:]
    o_ref[...] = (cb_ref[...].astype(F32) * y).astype(o_ref.dtype)


def conv_mixer(proj_main, conv_w):
    b, s, _ = proj_main.shape
    ts = _pick(s, (512, 256, 128))
    spec = lambda c: pl.BlockSpec((None, ts, MIX_W), lambda i, t, c=c: (i, t, c))
    return pl.pallas_call(
        _conv_mixer_kernel,
        grid=(b, s // ts),
        in_specs=[spec(0), spec(1), spec(2), pl.BlockSpec((3, MIX_W), lambda i, t: (0, 0))],
        out_specs=pl.BlockSpec((None, ts, MIX_W), lambda i, t: (i, t, 0)),
        out_shape=jax.ShapeDtypeStruct((b, s, MIX_W), BF16),
        scratch_shapes=[pltpu.VMEM((8, MIX_W), F32)],
        compiler_params=pltpu.CompilerParams(
            dimension_semantics=("parallel", "arbitrary"), vmem_limit_bytes=VMEM_LIMIT),
        name="conv_mixer",
    )(proj_main, proj_main, proj_main, conv_w)


def _sb_kernel(q_ref, k_ref, v_ref, o_ref, acc_ref, c_ref):
    tq = SB_BLOCK
    qi = pl.program_id(2)
    lane = lax.broadcasted_iota(jnp.int32, (tq, LANES), 1)
    q = q_ref[...]
    zero = jnp.zeros_like(q)
    qs = (jnp.where(lane < SB_HEAD_DIM, q, zero), jnp.where(lane >= SB_HEAD_DIM, q, zero))
    row = lax.broadcasted_iota(jnp.int32, (tq, tq), 0)
    col = lax.broadcasted_iota(jnp.int32, (tq, tq), 1)
    later = jnp.where(row > col, 1.0, 0.0).astype(BF16)
    causal = col < row
    scale = SB_HEAD_DIM ** -0.5

    def block(j, diag):
        start = pl.multiple_of(j * tq, tq)
        kb = k_ref[pl.ds(start, tq), :]
        vb = v_ref[pl.ds(start, tq), :]
        cmax = None
        for e in range(2):
            z = _dot_nt(qs[e], kb) * scale
            sp = jnp.maximum(z, 0.0) + jnp.log(1.0 + jnp.exp(-jnp.abs(z)))
            lsig = z - sp
            lnot = -sp
            if diag:
                lnot = jnp.where(causal, lnot, 0.0)
            c = c_ref[e]
            log_w = lsig + _dot_hilo(lnot, later) + c
            att = jnp.exp(log_w)
            if diag:
                att = jnp.where(causal, att, 0.0)
            acc_ref[e] += jnp.dot(att.astype(BF16), vb, preferred_element_type=F32)
            cn = c + jnp.sum(lnot, axis=1, keepdims=True)
            c_ref[e] = cn
            m = jnp.max(cn)
            cmax = m if cmax is None else jnp.maximum(cmax, m)
        return cmax

    acc_ref[...] = jnp.zeros_like(acc_ref)
    c_ref[...] = jnp.zeros_like(c_ref)
    m0 = block(qi, True)

    def cond(state):
        j, alive = state
        return jnp.logical_and(j >= 0, alive)

    def body(state):
        j, _ = state
        m = block(j, False)
        return j - 1, m >= SB_DEAD

    lax.while_loop(cond, body, (qi - 1, m0 >= SB_DEAD))
    o_ref[...] = jnp.where(lane < SB_HEAD_DIM, acc_ref[0], acc_ref[1]).astype(o_ref.dtype)


def stick_breaking(proj_main):
    b, s, _ = proj_main.shape
    tq = SB_BLOCK
    pairs = MIX_W // LANES
    off = 3 * MIX_W // LANES
    return pl.pallas_call(
        _sb_kernel,
        grid=(b, pairs, s // tq),
        in_specs=[pl.BlockSpec((None, tq, LANES), lambda i, p, t: (i, t, off + p)),
                  pl.BlockSpec((None, s, LANES), lambda i, p, t: (i, 0, off + pairs + p)),
                  pl.BlockSpec((None, s, LANES), lambda i, p, t: (i, 0, off + 2 * pairs + p))],
        out_specs=pl.BlockSpec((None, tq, LANES), lambda i, p, t: (i, t, p)),
        out_shape=jax.ShapeDtypeStruct((b, s, MIX_W), BF16),
        scratch_shapes=[pltpu.VMEM((2, tq, LANES), F32), pltpu.VMEM((2, tq, LANES), F32)],
        compiler_params=pltpu.CompilerParams(
            dimension_semantics=("parallel", "parallel", "arbitrary"), vmem_limit_bytes=VMEM_LIMIT),
        name="stick_breaking",
    )(proj_main, proj_main, proj_main)


def _stack_pair(x, lane_lo):
    z = jnp.zeros_like(x)
    return jnp.concatenate([jnp.where(lane_lo, x, z), jnp.where(lane_lo, z, x)], axis=0)


def _rwkv_kernel(*refs, vres):
    if vres:
        (p_ref, mu_ref, wl_ref, al_ref, gl_ref, vl_ref, vec_ref, seg_ref, vfirst_ref,
         y_ref, carry_ref, r_s, lw_s, k_s, v_s, kk_s, b_s, o_s, st_s) = refs
    else:
        (p_ref, mu_ref, wl_ref, al_ref, gl_ref, vec_ref, seg_ref,
         y_ref, vfirst_ref, carry_ref, r_s, lw_s, k_s, v_s, kk_s, b_s, o_s, st_s) = refs
    ts = p_ref.shape[0]
    ck = RWKV_CHUNK
    w_ = MIX_W

    @pl.when(pl.program_id(1) == 0)
    def _():
        carry_ref[...] = jnp.zeros_like(carry_ref)
        st_s[...] = jnp.zeros_like(st_s)

    p = p_ref[...]
    prev = _shift_rows(p, carry_ref[...], 1)
    carry_ref[...] = p[ts - 8:]
    p = p + mu_ref[...] * (prev - p)
    w0, a0, k_k, k_a, r_k, lnx_w, lnx_b, v0 = (vec_ref[i:i + 1] for i in range(8))
    seg = seg_ref[...]
    r = p[:, :w_]
    k = p[:, w_:2 * w_]
    v = p[:, 2 * w_:3 * w_]
    x_wa = p[:, 3 * w_:3 * w_ + LANES]
    x_g = p[:, 3 * w_ + LANES:3 * w_ + 2 * LANES]
    lw = -DECAY_SCALE * jax.nn.sigmoid(w0 + _dot(jnp.tanh(x_wa), wl_ref[...]))
    a = jax.nn.sigmoid(a0 + _dot(x_wa, al_ref[...]))
    if vres:
        x_v = p[:, 3 * w_ + 2 * LANES:3 * w_ + 3 * LANES]
        v = v + (vfirst_ref[...] - v) * jax.nn.sigmoid(v0 + _dot(x_v, vl_ref[...]))
    else:
        vfirst_ref[...] = v
    g = _dot(jax.nn.sigmoid(x_g), gl_ref[...])
    kk = k * k_k
    kk = kk / jnp.maximum(jnp.sqrt(_dot_hilo(kk * kk, seg)), 1e-12)
    k = k * (1.0 + (a - 1.0) * k_a)
    r_s[...] = r
    lw_s[...] = lw
    k_s[...] = k
    v_s[...] = v
    kk_s[...] = kk
    b_s[...] = kk * a

    rc = lax.broadcasted_iota(jnp.int32, (ck, ck), 0)
    cc = lax.broadcasted_iota(jnp.int32, (ck, ck), 1)
    tril = jnp.where(cc <= rc, 1.0, 0.0).astype(BF16)
    r2 = lax.broadcasted_iota(jnp.int32, (2 * ck, 2 * ck), 0)
    c2 = lax.broadcasted_iota(jnp.int32, (2 * ck, 2 * ck), 1)
    same = (r2 // ck) == (c2 // ck)
    strict = jnp.logical_and(same, c2 < r2)
    incl = jnp.logical_and(same, c2 <= r2)
    eye = jnp.where(r2 == c2, 1.0, 0.0)
    lane_lo = lax.broadcasted_iota(jnp.int32, (ck, RWKV_PAIR), 1) < RWKV_HEAD

    def chunk(ci, _):
        rows = pl.ds(pl.multiple_of(ci * ck, ck), ck)
        lw_c = lw_s[rows, :]
        lw_hi = lw_c.astype(BF16)
        lw_lo = (lw_c - lw_hi.astype(F32)).astype(BF16)
        cum = (jnp.dot(tril, lw_hi, preferred_element_type=F32)
               + jnp.dot(tril, lw_lo, preferred_element_type=F32))
        e_pos = jnp.exp(cum)
        e_neg = jnp.exp(-cum)
        total = cum[ck - 1:ck]
        e_end = jnp.exp(total - cum)
        a_all = kk_s[rows, :] * jnp.exp(cum - lw_c)
        b_all = b_s[rows, :]
        k_all = k_s[rows, :]
        r_all = r_s[rows, :] * e_pos
        v_all = v_s[rows, :]
        e_tot = jnp.exp(total)
        for pr in range(w_ // RWKV_PAIR):
            ls = slice(pr * RWKV_PAIR, (pr + 1) * RWKV_PAIR)
            a_m = _stack_pair(a_all[:, ls], lane_lo)
            b_m = _stack_pair(b_all[:, ls] * e_neg[:, ls], lane_lo)
            k_m = _stack_pair(k_all[:, ls] * e_neg[:, ls], lane_lo)
            r_m = _stack_pair(r_all[:, ls], lane_lo)
            v_m = _stack_pair(v_all[:, ls], lane_lo)
            b_end = _stack_pair(b_all[:, ls] * e_end[:, ls], lane_lo)
            k_end = _stack_pair(k_all[:, ls] * e_end[:, ls], lane_lo)
            l_ab = jnp.where(strict, _dot_nt(a_m, b_m), 0.0)
            l_ak = jnp.where(strict, _dot_nt(a_m, k_m), 0.0)
            m_rb = jnp.where(incl, _dot_nt(r_m, b_m), 0.0)
            m_rk = jnp.where(incl, _dot_nt(r_m, k_m), 0.0)
            t_inv = eye - l_ab
            pw = l_ab
            for _i in range(5):
                pw = _dot(pw, pw)
                t_inv = t_inv + _dot(t_inv, pw)
            w1 = _dot(t_inv, a_m)
            w2 = _dot(t_inv, _dot(l_ak, v_m))
            o0 = _dot(m_rk, v_m)
            kv = _dot(k_end.T, v_m)
            bt = b_end.T
            dec = jnp.broadcast_to(e_tot[:, ls], (2 * ck, RWKV_PAIR)).T
            st = st_s[pr]
            u = _dot(w1, st) + w2
            o = _dot(r_m, st) + o0 - _dot(m_rb, u)
            st_s[pr] = dec * st + kv - _dot(bt, u)
            o_s[rows, ls] = o[:ck] + o[ck:]
        return 0

    lax.fori_loop(0, ts // ck, chunk, 0)

    o = o_s[...]
    inv_n = 1.0 / RWKV_HEAD
    mean = _dot_hilo(o, seg) * inv_n
    d = o - mean
    var = _dot_hilo(d * d, seg) * inv_n
    gn = d * lax.rsqrt(var + GN_EPS) * lnx_w + lnx_b
    bonus = _dot_hilo(r * k * r_k, seg) * v
    y_ref[...] = ((gn + bonus) * g).astype(y_ref.dtype)


def rwkv7(p, mu, w_lora, a_lora, g_lora, v_lora, vecs, v_first):
    b, s, cols = p.shape
    vres = v_first is not None
    ts = _pick(s, (512, 256, 128, 64))
    head = jnp.arange(MIX_W) // RWKV_HEAD
    seg = (head[:, None] == head[None, :]).astype(BF16)
    tile = lambda c: pl.BlockSpec((None, ts, c), lambda i, t: (i, t, 0))
    full = lambda a: pl.BlockSpec(a.shape, lambda i, t: (0,) * a.ndim)
    ins = [p, mu, w_lora, a_lora, g_lora] + ([v_lora] if vres else []) + [vecs, seg]
    in_specs = [tile(cols)] + [full(a) for a in ins[1:]]
    y_shape = jax.ShapeDtypeStruct((b, s, MIX_W), BF16)
    if vres:
        ins.append(v_first)
        in_specs.append(tile(MIX_W))
        out_shape, out_specs = y_shape, tile(MIX_W)
    else:
        out_shape = (y_shape, jax.ShapeDtypeStruct((b, s, MIX_W), F32))
        out_specs = (tile(MIX_W), tile(MIX_W))
    pairs = MIX_W // RWKV_PAIR
    scratch = ([pltpu.VMEM((8, cols), F32)] + [pltpu.VMEM((ts, MIX_W), F32)] * 7
               + [pltpu.VMEM((pairs, RWKV_PAIR, RWKV_PAIR), F32)])
    out = pl.pallas_call(
        functools.partial(_rwkv_kernel, vres=vres),
        grid=(b, s // ts),
        in_specs=in_specs, out_specs=out_specs, out_shape=out_shape,
        scratch_shapes=scratch,
        compiler_params=pltpu.CompilerParams(
            dimension_semantics=("parallel", "arbitrary"), vmem_limit_bytes=VMEM_LIMIT),
        name="rwkv7",
    )(*ins)
    return (out, v_first) if vres else out


def _merge_kernel(yc_ref, ys_ref, yr_ref, gate_ref, gb_ref, bp_ref, wo_ref, h_ref, o_ref):
    merged = None
    for n, y_ref in enumerate((yc_ref, ys_ref, yr_ref)):
        branch = jnp.dot(y_ref[...], bp_ref[n], preferred_element_type=F32)
        gate = jax.nn.sigmoid(gate_ref[:, n * D_MODEL:(n + 1) * D_MODEL].astype(F32) + gb_ref[n:n + 1])
        merged = gate * branch if merged is None else merged + gate * branch
    o_ref[...] = h_ref[...] + _dot(merged, wo_ref[...])


def merge_branches(h, y_conv, y_sb, y_rwkv, proj_main, gate_b, branch_proj, w_mix_out):
    m = h.shape[0]
    tm = _pick(m, (512, 256, 128))
    gcols = N_BRANCH * D_MODEL
    row = lambda c: pl.BlockSpec((tm, c), lambda i: (i, 0))
    full = lambda a: pl.BlockSpec(a.shape, lambda i: (0,) * a.ndim)
    return pl.pallas_call(
        _merge_kernel,
        grid=(m // tm,),
        in_specs=[row(MIX_W), row(MIX_W), row(MIX_W),
                  pl.BlockSpec((tm, gcols), lambda i: (i, 1)),
                  full(gate_b), full(branch_proj), full(w_mix_out), row(D_MODEL)],
        out_specs=row(D_MODEL),
        out_shape=jax.ShapeDtypeStruct((m, D_MODEL), F32),
        compiler_params=pltpu.CompilerParams(
            dimension_semantics=("parallel",), vmem_limit_bytes=VMEM_LIMIT),
        name="merge_branches",
    )(y_conv, y_sb, y_rwkv, proj_main, gate_b, branch_proj, w_mix_out, h)


def _cross_attn_kernel(h_ref, g_ref, wq_ref, k_ref, v_ref, wo_ref, o_ref):
    h = h_ref[...]
    q = _dot(_rms(h, g_ref[...]), wq_ref[...]).astype(BF16)
    outs = []
    for hd in range(CA_HEADS):
        sl = slice(hd * CA_HEAD_DIM, (hd + 1) * CA_HEAD_DIM)
        sc = _dot_nt(q[:, sl], k_ref[:, sl]) * (CA_HEAD_DIM ** -0.5)
        e = jnp.exp(sc - jnp.max(sc, axis=-1, keepdims=True))
        probs = e / jnp.sum(e, axis=-1, keepdims=True)
        outs.append(_dot(probs, v_ref[:, sl]))
    o_ref[...] = h + _dot(jnp.concatenate(outs, axis=1), wo_ref[...])


def cross_attention(h, kv, gain, wq, wo):
    b, s, d = h.shape
    mem = kv.shape[1]
    tm = _pick(s, (512, 256, 128))
    full = lambda a: pl.BlockSpec(a.shape, lambda i, t: (0,) * a.ndim)
    return pl.pallas_call(
        _cross_attn_kernel,
        grid=(b, s // tm),
        in_specs=[pl.BlockSpec((None, tm, d), lambda i, t: (i, t, 0)),
                  full(gain), full(wq),
                  pl.BlockSpec((None, mem, d), lambda i, t: (i, 0, 0)),
                  pl.BlockSpec((None, mem, d), lambda i, t: (i, 0, 1)),
                  full(wo)],
        out_specs=pl.BlockSpec((None, tm, d), lambda i, t: (i, t, 0)),
        out_shape=jax.ShapeDtypeStruct((b, s, d), F32),
        compiler_params=pltpu.CompilerParams(
            dimension_semantics=("parallel", "parallel"), vmem_limit_bytes=VMEM_LIMIT),
        name="cross_attention",
    )(h, gain, wq, kv, kv, wo)


def _conv_ffn_kernel(h_ref, g_ref, wup_ref, cw_ref, cb_ref, wdn_ref, gf_ref, o_ref, carry_ref,
                     *, fc, final_norm):
    tm = h_ref.shape[0]
    nc = FFN_DIM // fc

    @pl.when(pl.program_id(1) == 0)
    def _():
        carry_ref[...] = jnp.zeros_like(carry_ref)

    h = h_ref[...]
    hn = _rms(h, g_ref[...]).astype(BF16)
    o_ref[...] = h
    for c in range(nc):
        halves = []
        for part in range(2):
            lo = part * FFN_DIM + c * fc
            u = jnp.dot(hn, wup_ref[:, lo:lo + fc], preferred_element_type=F32)
            prev8 = carry_ref[part * nc + c]
            cw = cw_ref[:, lo:lo + fc]
            halves.append(cw[2:3] * u + cw[1:2] * _shift_rows(u, prev8, 1)
                          + cw[0:1] * _shift_rows(u, prev8, 2) + cb_ref[:, lo:lo + fc])
            carry_ref[part * nc + c] = u[tm - 8:]
        act = jax.nn.silu(halves[0]) * halves[1]
        o_ref[...] += _dot(act, wdn_ref[c * fc:(c + 1) * fc, :])
    if final_norm:
        o_ref[...] = _rms(o_ref[...], gf_ref[...])


def conv_ffn(h, gain, w_up, conv_w, conv_b, w_down, gain_final, final_norm):
    b, s, d = h.shape
    tm = _pick(s, (512, 256, 128))
    fc = 256
    full = lambda a: pl.BlockSpec(a.shape, lambda i, t: (0,) * a.ndim, pipeline_mode=pl.Buffered(1))
    return pl.pallas_call(
        functools.partial(_conv_ffn_kernel, fc=fc, final_norm=final_norm),
        grid=(b, s // tm),
        in_specs=[pl.BlockSpec((None, tm, d), lambda i, t: (i, t, 0)),
                  full(gain), full(w_up), full(conv_w), full(conv_b), full(w_down), full(gain_final)],
        out_specs=pl.BlockSpec((None, tm, d), lambda i, t: (i, t, 0)),
        out_shape=jax.ShapeDtypeStruct((b, s, d), F32),
        scratch_shapes=[pltpu.VMEM((2 * FFN_DIM // fc, 8, fc), F32)],
        compiler_params=pltpu.CompilerParams(
            dimension_semantics=("parallel", "arbitrary"), vmem_limit_bytes=FFN_VMEM_LIMIT),
        name="conv_ffn",
    )(h, gain, w_up, conv_w, conv_b, w_down, gain_final)


def _pad_rows(w, rows):
    return jnp.concatenate([w, jnp.zeros((rows - w.shape[0],) + w.shape[1:], w.dtype)], axis=0)


def kernel(x, mem, norm_mix, w_comb, conv_w, mu_rwkv, w0, w_lora, a0, a_lora, g_lora, k_k, k_a, r_k, lnx_w, lnx_b, w_vres, mu_vres, v0, v_lora, branch_proj, gate_b, w_mix_out, norm_ca, norm_mem, ca_wq, ca_wkv, ca_wo, norm_ffn, ffn_up, ffn_conv_w, ffn_conv_b, ffn_down, norm_final):
    b, s, d = x.shape
    depth = w_comb.shape[0]
    mem_len = mem.shape[1]
    h = x
    v_first = None
    zero_lora = jnp.zeros((DECAY_LORA, MIX_W), F32)
    for l in range(depth):
        h2 = h.reshape(b * s, d)
        w_main = w_comb[l, :, :MAIN_COLS].astype(BF16)
        w_rwkv = w_comb[l, :, MAIN_COLS:]
        mu = mu_rwkv[l]
        if l > 0:
            pad = LANES - VRES_LORA
            w_rwkv = jnp.concatenate([w_rwkv, w_vres[l - 1], jnp.zeros((d, pad), F32)], axis=1)
            mu = jnp.concatenate([mu, mu_vres[l - 1], jnp.zeros((pad,), F32)])
        proj_main = norm_matmul(h2, norm_mix[l], w_main, BF16).reshape(b, s, MAIN_COLS)
        proj_rwkv = norm_matmul(h2, norm_mix[l], w_rwkv.astype(BF16), F32).reshape(b, s, -1)

        y_conv = conv_mixer(proj_main, conv_w[l])
        y_sb = stick_breaking(proj_main)
        vecs = jnp.stack([w0[l], a0[l], k_k[l], k_a[l], r_k[l].reshape(-1), lnx_w[l], lnx_b[l],
                          v0[l - 1] if l > 0 else jnp.zeros((MIX_W,), F32)])
        y_rwkv, v_first = rwkv7(
            proj_rwkv, mu.reshape(1, -1),
            jnp.concatenate([w_lora[l], zero_lora], axis=0).astype(BF16),
            jnp.concatenate([zero_lora, a_lora[l]], axis=0).astype(BF16),
            g_lora[l].astype(BF16),
            _pad_rows(v_lora[l - 1], LANES).astype(BF16) if l > 0 else None,
            vecs, v_first)

        h2 = merge_branches(h2, y_conv.reshape(b * s, MIX_W), y_sb.reshape(b * s, MIX_W),
                            y_rwkv.reshape(b * s, MIX_W), proj_main.reshape(b * s, MAIN_COLS),
                            gate_b[l], branch_proj[l].astype(BF16), w_mix_out[l].astype(BF16))

        kv = norm_matmul(mem.reshape(b * mem_len, d), norm_mem[l], ca_wkv[l].astype(BF16), BF16)
        h = cross_attention(h2.reshape(b, s, d), kv.reshape(b, mem_len, 2 * d),
                            norm_ca[l].reshape(1, d), ca_wq[l].astype(BF16), ca_wo[l].astype(BF16))

        h = conv_ffn(h, norm_ffn[l].reshape(1, d), ffn_up[l].astype(BF16), ffn_conv_w[l],
                     ffn_conv_b[l].reshape(1, -1), ffn_down[l].astype(BF16),
                     norm_final.reshape(1, d), final_norm=(l == depth - 1))
    return h
```

```python
import functools
import math

import jax
import jax.numpy as jnp
from jax import lax
from jax.experimental import pallas as pl
from jax.experimental.pallas import tpu as pltpu

F32 = jnp.float32
BF16 = jnp.bfloat16

D_MODEL = 1024
MIX_W = 512
N_BRANCH = 3
SB_HEADS = 8
SB_HEAD_DIM = MIX_W // SB_HEADS
RWKV_HEADS = 8
RWKV_HEAD = MIX_W // RWKV_HEADS
DECAY_LORA = 64
ICLR_LORA = 64
GATE_LORA = 128
VRES_LORA = 32
DECAY_SCALE = math.exp(-0.5)
CA_HEADS = 4
CA_HEAD_DIM = D_MODEL // CA_HEADS
FFN_DIM = 2816
RMS_EPS = 1e-6
GN_EPS = 64e-5

MAIN_COLS = 3 * MIX_W + 3 * MIX_W + N_BRANCH * D_MODEL
RWKV_COLS = 3 * MIX_W + DECAY_LORA + ICLR_LORA + GATE_LORA
LANES = 128
RWKV_COLS_VRES = RWKV_COLS + LANES
VMEM_LIMIT = 48 * 1024 * 1024
FFN_VMEM_LIMIT = 56 * 1024 * 1024

SB_BLOCK = 128
SB_DEAD = -104.0
RWKV_CHUNK = 64
RWKV_PAIR = 2 * RWKV_HEAD
RWKV_CHUNKS_PER_ITER = 4
MXU_WIDTH = 256


def _dot(a, b):
    return jnp.dot(a.astype(BF16), b.astype(BF16), preferred_element_type=F32)


def _dot_nt(a, b):
    return lax.dot_general(a.astype(BF16), b.astype(BF16), (((1,), (1,)), ((), ())),
                           preferred_element_type=F32)


def _head_sums(x, seg):
    xb = x.astype(BF16)
    n = seg.shape[0]
    return jnp.concatenate([jnp.dot(xb[:, i:i + n], seg, preferred_element_type=F32)
                            for i in range(0, x.shape[1], n)], axis=1)


def _rms(x, g):
    ms = jnp.mean(x * x, axis=-1, keepdims=True)
    return x * lax.rsqrt(ms + RMS_EPS) * g


def _shift_rows(u, prev8, n):
    rows = lax.broadcasted_iota(jnp.int32, u.shape, 0)
    r = pltpu.roll(u, n, 0)
    if n == 1:
        return jnp.where(rows == 0, prev8[7:8], r)
    return jnp.where(rows == 0, prev8[6:7], jnp.where(rows == 1, prev8[7:8], r))


def _pick(n, cands):
    for c in cands:
        if n % c == 0:
            return c
    raise ValueError(f"no tile for {n} in {cands}")


def _norm_mm_kernel(x_ref, g_ref, w_ref, o_ref, xn_ref):
    @pl.when(pl.program_id(1) == 0)
    def _():
        xn_ref[...] = _rms(x_ref[...], g_ref[...]).astype(BF16)

    o_ref[...] = jnp.dot(xn_ref[...], w_ref[...], preferred_element_type=F32).astype(o_ref.dtype)


def norm_matmul(x, gain, w, out_dtype):
    m, k = x.shape
    n = w.shape[1]
    tm = _pick(m, (1024, 512, 256, 128))
    tn = _pick(n, (512, 384, 256, 128))
    return pl.pallas_call(
        _norm_mm_kernel,
        grid=(m // tm, n // tn),
        in_specs=[pl.BlockSpec((tm, k), lambda i, j: (i, 0)),
                  pl.BlockSpec((1, k), lambda i, j: (0, 0)),
                  pl.BlockSpec((k, tn), lambda i, j: (0, j))],
        out_specs=pl.BlockSpec((tm, tn), lambda i, j: (i, j)),
        out_shape=jax.ShapeDtypeStruct((m, n), out_dtype),
        scratch_shapes=[pltpu.VMEM((tm, k), BF16)],
        compiler_params=pltpu.CompilerParams(
            dimension_semantics=("parallel", "arbitrary"), vmem_limit_bytes=VMEM_LIMIT),
        name="norm_matmul",
    )(x, gain.reshape(1, k), w)


def _in_proj_kernel(x_ref, g_ref, wm_ref, wr_ref, om_ref, or_ref):
    xn = _rms(x_ref[...], g_ref[...]).astype(BF16)
    for w_ref, o_ref in ((wm_ref, om_ref), (wr_ref, or_ref)):
        cols = w_ref.shape[1]
        for lo in range(0, cols, 2 * MXU_WIDTH):
            n = min(2 * MXU_WIDTH, cols - lo)
            o_ref[:, lo:lo + n] = jnp.dot(xn, w_ref[:, lo:lo + n],
                                          preferred_element_type=F32).astype(o_ref.dtype)


def input_projection(x, gain, w_main, w_rwkv):
    m, k = x.shape
    tm = _pick(m, (512, 256, 128))
    nm, nr = w_main.shape[1], w_rwkv.shape[1]
    gain = gain.reshape(1, k)
    full = lambda a: pl.BlockSpec(a.shape, lambda i: (0,) * a.ndim, pipeline_mode=pl.Buffered(1))
    row = lambda c: pl.BlockSpec((tm, c), lambda i: (i, 0))
    return pl.pallas_call(
        _in_proj_kernel,
        grid=(m // tm,),
        in_specs=[row(k), full(gain), full(w_main), full(w_rwkv)],
        out_specs=(row(nm), row(nr)),
        out_shape=(jax.ShapeDtypeStruct((m, nm), BF16), jax.ShapeDtypeStruct((m, nr), F32)),
        compiler_params=pltpu.CompilerParams(
            dimension_semantics=("parallel",), vmem_limit_bytes=FFN_VMEM_LIMIT),
        name="input_projection",
    )(x, gain, w_main, w_rwkv)


def _conv_mixer_kernel(cb_ref, cc_ref, cx_ref, w_ref, o_ref, carry_ref):
    @pl.when(pl.program_id(1) == 0)
    def _():
        carry_ref[...] = jnp.zeros_like(carry_ref)

    u = cc_ref[...].astype(F32) * cx_ref[...].astype(F32)
    prev8 = carry_ref[...]
    y = w_ref[2:3] * u + w_ref[1:2] * _shift_rows(u, prev8, 1) + w_ref[0:1] * _shift_rows(u, prev8, 2)
    carry_ref[...] = u[u.shape[0] - 8:]
    o_ref[...] = (cb_ref[...].astype(F32) * y).astype(o_ref.dtype)


def conv_mixer(proj_main, conv_w):
    b, s, _ = proj_main.shape
    ts = _pick(s, (512, 256, 128))
    spec = lambda c: pl.BlockSpec((None, ts, MIX_W), lambda i, t, c=c: (i, t, c))
    return pl.pallas_call(
        _conv_mixer_kernel,
        grid=(b, s // ts),
        in_specs=[spec(0), spec(1), spec(2), pl.BlockSpec((3, MIX_W), lambda i, t: (0, 0))],
        out_specs=pl.BlockSpec((None, ts, MIX_W), lambda i, t: (i, t, 0)),
        out_shape=jax.ShapeDtypeStruct((b, s, MIX_W), BF16),
        scratch_shapes=[pltpu.VMEM((8, MIX_W), F32)],
        compiler_params=pltpu.CompilerParams(
            dimension_semantics=("parallel", "arbitrary"), vmem_limit_bytes=VMEM_LIMIT),
        name="conv_mixer",
    )(proj_main, proj_main, proj_main, conv_w)


def _stack_pair(x, lane_lo):
    z = jnp.zeros_like(x)
    return jnp.concatenate([jnp.where(lane_lo, x, z), jnp.where(lane_lo, z, x)], axis=0)


def _sb_kernel(q_ref, k_ref, v_ref, later_ref, o_ref, acc_ref, c_ref):
    tq = SB_BLOCK
    prs = range(MIX_W // LANES)
    qi = pl.program_id(1)
    lane_lo = lax.broadcasted_iota(jnp.int32, (tq, LANES), 1) < SB_HEAD_DIM
    row = lax.broadcasted_iota(jnp.int32, (tq, 2 * tq), 0)
    col = lax.broadcasted_iota(jnp.int32, (tq, 2 * tq), 1)
    first = col < tq
    causal = jnp.where(first, col, col - tq) < row
    q = [q_ref[:, p * LANES:(p + 1) * LANES] * (SB_HEAD_DIM ** -0.5) for p in prs]
    later = later_ref[...]

    def block(j, diag):
        start = pl.multiple_of(j * tq, tq)
        kb = k_ref[pl.ds(start, tq), :]
        vb = v_ref[pl.ds(start, tq), :]
        ks = [_stack_pair(kb[:, p * LANES:(p + 1) * LANES], lane_lo) for p in prs]
        vs = [_stack_pair(vb[:, p * LANES:(p + 1) * LANES], lane_lo) for p in prs]
        z = [_dot_nt(q[p], ks[p]) for p in prs]
        sp = [jnp.maximum(z[p], 0.0) + jnp.log(1.0 + jnp.exp(-jnp.abs(z[p]))) for p in prs]
        lsig = [z[p] - sp[p] for p in prs]
        if diag:
            sp = [jnp.where(causal, sp[p], 0.0) for p in prs]
        hi = [sp[p].astype(BF16) for p in prs]
        lo = [(sp[p] - hi[p].astype(F32)).astype(BF16) for p in prs]
        suf = [jnp.dot(hi[p], later, preferred_element_type=F32)
               + jnp.dot(lo[p], later, preferred_element_type=F32) for p in prs]
        c = [c_ref[p] for p in prs]
        att = [jnp.exp(lsig[p] - suf[p] - c[p]) for p in prs]
        if diag:
            att = [jnp.where(causal, att[p], 0.0) for p in prs]
        cmin = None
        for p in prs:
            acc_ref[p] += jnp.dot(att[p].astype(BF16), vs[p], preferred_element_type=F32)
            cn = c[p] + jnp.where(first, jnp.sum(sp[p][:, :tq], axis=1, keepdims=True),
                                  jnp.sum(sp[p][:, tq:], axis=1, keepdims=True))
            c_ref[p] = cn
            cmin = cn if cmin is None else jnp.minimum(cmin, cn)
        return jnp.min(cmin)

    acc_ref[...] = jnp.zeros_like(acc_ref)
    c_ref[...] = jnp.zeros_like(c_ref)
    m0 = block(qi, True)

    def cond(state):
        j, alive = state
        return jnp.logical_and(j >= 0, alive)

    def body(state):
        j, _ = state
        m = block(j, False)
        return j - 1, m <= -SB_DEAD

    lax.while_loop(cond, body, (qi - 1, m0 <= -SB_DEAD))
    o_ref[...] = jnp.concatenate([acc_ref[p] for p in prs], axis=1).astype(o_ref.dtype)


def stick_breaking(proj_main):
    b, s, _ = proj_main.shape
    tq = SB_BLOCK
    pairs = MIX_W // LANES
    half = jnp.arange(2 * tq) // tq
    pos = jnp.arange(2 * tq)
    later = ((half[:, None] == half[None, :]) & (pos[:, None] > pos[None, :])).astype(BF16)
    return pl.pallas_call(
        _sb_kernel,
        grid=(b, s // tq),
        in_specs=[pl.BlockSpec((None, tq, MIX_W), lambda i, t: (i, t, 3)),
                  pl.BlockSpec((None, s, MIX_W), lambda i, t: (i, 0, 4)),
                  pl.BlockSpec((None, s, MIX_W), lambda i, t: (i, 0, 5)),
                  pl.BlockSpec((2 * tq, 2 * tq), lambda i, t: (0, 0))],
        out_specs=pl.BlockSpec((None, tq, MIX_W), lambda i, t: (i, t, 0)),
        out_shape=jax.ShapeDtypeStruct((b, s, MIX_W), BF16),
        scratch_shapes=[pltpu.VMEM((pairs, tq, LANES), F32), pltpu.VMEM((pairs, tq, 2 * tq), F32)],
        compiler_params=pltpu.CompilerParams(
            dimension_semantics=("parallel", "arbitrary"), vmem_limit_bytes=VMEM_LIMIT),
        name="stick_breaking",
    )(proj_main, proj_main, proj_main, later)


def _rwkv_kernel(*refs, vres):
    if vres:
        (p_ref, mu_ref, wl_ref, al_ref, gl_ref, vl_ref, vec_ref, seg_ref, vfirst_ref,
         y_ref, carry_ref, r_s, lw_s, k_s, v_s, kk_s, b_s, o_s, st_s) = refs
    else:
        (p_ref, mu_ref, wl_ref, al_ref, gl_ref, vec_ref, seg_ref,
         y_ref, vfirst_ref, carry_ref, r_s, lw_s, k_s, v_s, kk_s, b_s, o_s, st_s) = refs
    ts = p_ref.shape[0]
    ck = RWKV_CHUNK
    w_ = MIX_W

    @pl.when(pl.program_id(1) == 0)
    def _():
        carry_ref[...] = jnp.zeros_like(carry_ref)
        st_s[...] = jnp.zeros_like(st_s)

    p = p_ref[...]
    prev = _shift_rows(p, carry_ref[...], 1)
    carry_ref[...] = p[ts - 8:]
    p = p + mu_ref[...] * (prev - p)
    w0, a0, k_k, k_a, r_k, lnx_w, lnx_b, v0 = (vec_ref[i:i + 1] for i in range(8))
    seg = seg_ref[...]
    r = p[:, :w_]
    k = p[:, w_:2 * w_]
    v = p[:, 2 * w_:3 * w_]
    x_wa = p[:, 3 * w_:3 * w_ + LANES]
    x_g = p[:, 3 * w_ + LANES:3 * w_ + 2 * LANES]
    lw = -DECAY_SCALE * jax.nn.sigmoid(w0 + _dot(jnp.tanh(x_wa), wl_ref[...]))
    a = jax.nn.sigmoid(a0 + _dot(x_wa, al_ref[...]))
    if vres:
        x_v = p[:, 3 * w_ + 2 * LANES:3 * w_ + 3 * LANES]
        v = v + (vfirst_ref[...] - v) * jax.nn.sigmoid(v0 + _dot(x_v, vl_ref[...]))
    else:
        vfirst_ref[...] = v
    g = _dot(jax.nn.sigmoid(x_g), gl_ref[...])
    kk = k * k_k
    kk = kk / jnp.maximum(jnp.sqrt(_head_sums(kk * kk, seg)), 1e-12)
    k = k * (1.0 + (a - 1.0) * k_a)
    r_s[...] = r
    lw_s[...] = lw
    k_s[...] = k
    v_s[...] = v
    kk_s[...] = kk
    b_s[...] = kk * a

    rc = lax.broadcasted_iota(jnp.int32, (ck, ck), 0)
    cc = lax.broadcasted_iota(jnp.int32, (ck, ck), 1)
    tril = jnp.where(cc <= rc, 1.0, 0.0).astype(BF16)
    r2 = lax.broadcasted_iota(jnp.int32, (2 * ck, 2 * ck), 0)
    c2 = lax.broadcasted_iota(jnp.int32, (2 * ck, 2 * ck), 1)
    same = (r2 // ck) == (c2 // ck)
    strict = jnp.logical_and(same, c2 < r2)
    incl = jnp.logical_and(same, c2 <= r2)
    eye = jnp.where(r2 == c2, 1.0, 0.0)
    lane_lo = lax.broadcasted_iota(jnp.int32, (ck, RWKV_PAIR), 1) < RWKV_HEAD

    npair = w_ // RWKV_PAIR
    nch = RWKV_CHUNKS_PER_ITER
    sl = [slice(p * RWKV_PAIR, (p + 1) * RWKV_PAIR) for p in range(npair)]

    def chunks(ci, _):
        units = [(j, p) for j in range(nch) for p in range(npair)]
        a_m, b_m, k_m, r_m, v_m, b_end, k_end, dec, rows = [], [], [], [], [], [], [], [], []
        for j in range(nch):
            rws = pl.ds(pl.multiple_of((ci * nch + j) * ck, ck), ck)
            rows.append(rws)
            lw_c = lw_s[rws, :]
            lw_hi = lw_c.astype(BF16)
            lw_lo = (lw_c - lw_hi.astype(F32)).astype(BF16)
            cum = (jnp.dot(tril, lw_hi, preferred_element_type=F32)
                   + jnp.dot(tril, lw_lo, preferred_element_type=F32))
            e_neg = jnp.exp(-cum)
            total = cum[ck - 1:ck]
            e_end = jnp.exp(total - cum)
            e_tot = jnp.exp(total)
            a_all = kk_s[rws, :] * jnp.exp(cum - lw_c)
            b_all = b_s[rws, :]
            k_all = k_s[rws, :]
            r_all = r_s[rws, :] * jnp.exp(cum)
            v_all = v_s[rws, :]
            stk = lambda x: [_stack_pair(x[:, ls], lane_lo).astype(BF16) for ls in sl]
            a_m += stk(a_all)
            b_m += stk(b_all * e_neg)
            k_m += stk(k_all * e_neg)
            r_m += stk(r_all)
            v_m += stk(v_all)
            b_end += stk(b_all * e_end)
            k_end += stk(k_all * e_end)
            dec += [jnp.broadcast_to(e_tot[:, ls], (2 * ck, RWKV_PAIR)).T for ls in sl]
        un = range(len(units))
        l_ab = [jnp.where(strict, _dot_nt(a_m[i], b_m[i]), 0.0) for i in un]
        l_ak = [jnp.where(strict, _dot_nt(a_m[i], k_m[i]), 0.0) for i in un]
        m_rb = [jnp.where(incl, _dot_nt(r_m[i], b_m[i]), 0.0).astype(BF16) for i in un]
        m_rk = [jnp.where(incl, _dot_nt(r_m[i], k_m[i]), 0.0) for i in un]
        t_inv = [eye - l_ab[i] for i in un]
        pw = [l_ab[i].astype(BF16) for i in un]
        for _i in range(5):
            pw = [_dot(pw[i], pw[i]).astype(BF16) for i in un]
            t_inv = [t_inv[i] + _dot(t_inv[i], pw[i]) for i in un]
        t_inv = [t_inv[i].astype(BF16) for i in un]
        lv = [_dot(l_ak[i], v_m[i]) for i in un]
        w1 = [_dot(t_inv[i], a_m[i]).astype(BF16) for i in un]
        w2 = [_dot(t_inv[i], lv[i]) for i in un]
        o0 = [_dot(m_rk[i], v_m[i]) for i in un]
        kv = [_dot(k_end[i].astype(F32).T, v_m[i]) for i in un]
        bt = [b_end[i].astype(F32).T.astype(BF16) for i in un]
        st = [st_s[p] for p in range(npair)]
        for j in range(nch):
            ids = [j * npair + p for p in range(npair)]
            st_b = [st[p].astype(BF16) for p in range(npair)]
            u = [_dot(w1[i], st_b[p]) + w2[i] for p, i in enumerate(ids)]
            o = [_dot(r_m[i], st_b[p]) + o0[i] - _dot(m_rb[i], u[p]) for p, i in enumerate(ids)]
            st = [dec[i] * st[p] + kv[i] - _dot(bt[i], u[p]) for p, i in enumerate(ids)]
            for p in range(npair):
                o_s[rows[j], sl[p]] = o[p][:ck] + o[p][ck:]
        for p in range(npair):
            st_s[p] = st[p]
        return 0

    lax.fori_loop(0, ts // (ck * nch), chunks, 0)

    o = o_s[...]
    inv_n = 1.0 / RWKV_HEAD
    mean = _head_sums(o, seg) * inv_n
    d = o - mean
    var = _head_sums(d * d, seg) * inv_n
    gn = d * lax.rsqrt(var + GN_EPS) * lnx_w + lnx_b
    bonus = _head_sums(r * k * r_k, seg) * v
    y_ref[...] = ((gn + bonus) * g).astype(y_ref.dtype)


def rwkv7(p, mu, w_lora, a_lora, g_lora, v_lora, vecs, v_first):
    b, s, cols = p.shape
    vres = v_first is not None
    ts = _pick(s, (512, 256, 128, 64))
    head = jnp.arange(MXU_WIDTH) // RWKV_HEAD
    seg = (head[:, None] == head[None, :]).astype(BF16)
    tile = lambda c: pl.BlockSpec((None, ts, c), lambda i, t: (i, t, 0))
    full = lambda a: pl.BlockSpec(a.shape, lambda i, t: (0,) * a.ndim)
    ins = [p, mu, w_lora, a_lora, g_lora] + ([v_lora] if vres else []) + [vecs, seg]
    in_specs = [tile(cols)] + [full(a) for a in ins[1:]]
    y_shape = jax.ShapeDtypeStruct((b, s, MIX_W), BF16)
    if vres:
        ins.append(v_first)
        in_specs.append(tile(MIX_W))
        out_shape, out_specs = y_shape, tile(MIX_W)
    else:
        out_shape = (y_shape, jax.ShapeDtypeStruct((b, s, MIX_W), F32))
        out_specs = (tile(MIX_W), tile(MIX_W))
    pairs = MIX_W // RWKV_PAIR
    scratch = ([pltpu.VMEM((8, cols), F32)] + [pltpu.VMEM((ts, MIX_W), F32)] * 7
               + [pltpu.VMEM((pairs, RWKV_PAIR, RWKV_PAIR), F32)])
    out = pl.pallas_call(
        functools.partial(_rwkv_kernel, vres=vres),
        grid=(b, s // ts),
        in_specs=in_specs, out_specs=out_specs, out_shape=out_shape,
        scratch_shapes=scratch,
        compiler_params=pltpu.CompilerParams(
            dimension_semantics=("parallel", "arbitrary"), vmem_limit_bytes=VMEM_LIMIT),
        name="rwkv7",
    )(*ins)
    return (out, v_first) if vres else out


def _merge_kernel(yc_ref, ys_ref, yr_ref, gate_ref, gb_ref, bp_ref, wo_ref, h_ref, o_ref):
    merged = None
    for n, y_ref in enumerate((yc_ref, ys_ref, yr_ref)):
        branch = jnp.dot(y_ref[...], bp_ref[n], preferred_element_type=F32)
        gate = jax.nn.sigmoid(gate_ref[:, n * D_MODEL:(n + 1) * D_MODEL].astype(F32) + gb_ref[n:n + 1])
        merged = gate * branch if merged is None else merged + gate * branch
    o_ref[...] = h_ref[...] + _dot(merged, wo_ref[...])


def merge_branches(h, y_conv, y_sb, y_rwkv, proj_main, gate_b, branch_proj, w_mix_out):
    m = h.shape[0]
    tm = _pick(m, (512, 256, 128))
    gcols = N_BRANCH * D_MODEL
    row = lambda c: pl.BlockSpec((tm, c), lambda i: (i, 0))
    full = lambda a: pl.BlockSpec(a.shape, lambda i: (0,) * a.ndim)
    return pl.pallas_call(
        _merge_kernel,
        grid=(m // tm,),
        in_specs=[row(MIX_W), row(MIX_W), row(MIX_W),
                  pl.BlockSpec((tm, gcols), lambda i: (i, 1)),
                  full(gate_b), full(branch_proj), full(w_mix_out), row(D_MODEL)],
        out_specs=row(D_MODEL),
        out_shape=jax.ShapeDtypeStruct((m, D_MODEL), F32),
        compiler_params=pltpu.CompilerParams(
            dimension_semantics=("parallel",), vmem_limit_bytes=VMEM_LIMIT),
        name="merge_branches",
    )(y_conv, y_sb, y_rwkv, proj_main, gate_b, branch_proj, w_mix_out, h)


def _cross_attn_kernel(h_ref, g_ref, wq_ref, k_ref, v_ref, wo_ref, o_ref):
    h = h_ref[...]
    q = _dot(_rms(h, g_ref[...]), wq_ref[...]).astype(BF16)
    outs = []
    for hd in range(CA_HEADS):
        sl = slice(hd * CA_HEAD_DIM, (hd + 1) * CA_HEAD_DIM)
        sc = _dot_nt(q[:, sl], k_ref[:, sl]) * (CA_HEAD_DIM ** -0.5)
        e = jnp.exp(sc - jnp.max(sc, axis=-1, keepdims=True))
        probs = e / jnp.sum(e, axis=-1, keepdims=True)
        outs.append(_dot(probs, v_ref[:, sl]))
    o_ref[...] = h + _dot(jnp.concatenate(outs, axis=1), wo_ref[...])


def cross_attention(h, kv, gain, wq, wo):
    b, s, d = h.shape
    mem = kv.shape[1]
    tm = _pick(s, (512, 256, 128))
    full = lambda a: pl.BlockSpec(a.shape, lambda i, t: (0,) * a.ndim)
    return pl.pallas_call(
        _cross_attn_kernel,
        grid=(b, s // tm),
        in_specs=[pl.BlockSpec((None, tm, d), lambda i, t: (i, t, 0)),
                  full(gain), full(wq),
                  pl.BlockSpec((None, mem, d), lambda i, t: (i, 0, 0)),
                  pl.BlockSpec((None, mem, d), lambda i, t: (i, 0, 1)),
                  full(wo)],
        out_specs=pl.BlockSpec((None, tm, d), lambda i, t: (i, t, 0)),
        out_shape=jax.ShapeDtypeStruct((b, s, d), F32),
        compiler_params=pltpu.CompilerParams(
            dimension_semantics=("parallel", "parallel"), vmem_limit_bytes=VMEM_LIMIT),
        name="cross_attention",
    )(h, gain, wq, kv, kv, wo)


def _conv_ffn_kernel(h_ref, g_ref, wup_ref, cw_ref, cb_ref, wdn_ref, gf_ref, o_ref, carry_ref,
                     act_ref, *, fc, final_norm):
    tm = h_ref.shape[0]
    nc = FFN_DIM // fc

    @pl.when(pl.program_id(1) == 0)
    def _():
        carry_ref[...] = jnp.zeros_like(carry_ref)

    h = h_ref[...]
    hn = _rms(h, g_ref[...]).astype(BF16)
    for c in range(nc):
        halves = []
        for part in range(2):
            lo = part * FFN_DIM + c * fc
            u = jnp.dot(hn, wup_ref[:, lo:lo + fc], preferred_element_type=F32)
            prev8 = carry_ref[part * nc + c]
            cw = cw_ref[:, lo:lo + fc]
            halves.append(cw[2:3] * u + cw[1:2] * _shift_rows(u, prev8, 1)
                          + cw[0:1] * _shift_rows(u, prev8, 2) + cb_ref[:, lo:lo + fc])
            carry_ref[part * nc + c] = u[tm - 8:]
        act_ref[:, c * fc:(c + 1) * fc] = (jax.nn.silu(halves[0]) * halves[1]).astype(BF16)
    out = h + jnp.dot(act_ref[...], wdn_ref[...], preferred_element_type=F32)
    o_ref[...] = _rms(out, gf_ref[...]) if final_norm else out


def conv_ffn(h, gain, w_up, conv_w, conv_b, w_down, gain_final, final_norm):
    b, s, d = h.shape
    tm = _pick(s, (512, 256, 128))
    fc = 256
    full = lambda a: pl.BlockSpec(a.shape, lambda i, t: (0,) * a.ndim, pipeline_mode=pl.Buffered(1))
    return pl.pallas_call(
        functools.partial(_conv_ffn_kernel, fc=fc, final_norm=final_norm),
        grid=(b, s // tm),
        in_specs=[pl.BlockSpec((None, tm, d), lambda i, t: (i, t, 0)),
                  full(gain), full(w_up), full(conv_w), full(conv_b), full(w_down), full(gain_final)],
        out_specs=pl.BlockSpec((None, tm, d), lambda i, t: (i, t, 0)),
        out_shape=jax.ShapeDtypeStruct((b, s, d), F32),
        scratch_shapes=[pltpu.VMEM((2 * FFN_DIM // fc, 8, fc), F32), pltpu.VMEM((tm, FFN_DIM), BF16)],
        compiler_params=pltpu.CompilerParams(
            dimension_semantics=("parallel", "arbitrary"), vmem_limit_bytes=FFN_VMEM_LIMIT),
        name="conv_ffn",
    )(h, gain, w_up, conv_w, conv_b, w_down, gain_final)


def _pad_rows(w, rows):
    return jnp.concatenate([w, jnp.zeros((rows - w.shape[0],) + w.shape[1:], w.dtype)], axis=0)


def kernel(x, mem, norm_mix, w_comb, conv_w, mu_rwkv, w0, w_lora, a0, a_lora, g_lora, k_k, k_a, r_k, lnx_w, lnx_b, w_vres, mu_vres, v0, v_lora, branch_proj, gate_b, w_mix_out, norm_ca, norm_mem, ca_wq, ca_wkv, ca_wo, norm_ffn, ffn_up, ffn_conv_w, ffn_conv_b, ffn_down, norm_final):
    b, s, d = x.shape
    depth = w_comb.shape[0]
    mem_len = mem.shape[1]
    h = x
    v_first = None
    zero_lora = jnp.zeros((DECAY_LORA, MIX_W), F32)
    for l in range(depth):
        h2 = h.reshape(b * s, d)
        w_main = w_comb[l, :, :MAIN_COLS].astype(BF16)
        w_rwkv = w_comb[l, :, MAIN_COLS:]
        mu = mu_rwkv[l]
        if l > 0:
            pad = LANES - VRES_LORA
            w_rwkv = jnp.concatenate([w_rwkv, w_vres[l - 1], jnp.zeros((d, pad), F32)], axis=1)
            mu = jnp.concatenate([mu, mu_vres[l - 1], jnp.zeros((pad,), F32)])
        proj_main, proj_rwkv = input_projection(h2, norm_mix[l], w_main, w_rwkv.astype(BF16))
        proj_main = proj_main.reshape(b, s, MAIN_COLS)
        proj_rwkv = proj_rwkv.reshape(b, s, -1)

        y_conv = conv_mixer(proj_main, conv_w[l])
        y_sb = stick_breaking(proj_main)
        vecs = jnp.stack([w0[l], a0[l], k_k[l], k_a[l], r_k[l].reshape(-1), lnx_w[l], lnx_b[l],
                          v0[l - 1] if l > 0 else jnp.zeros((MIX_W,), F32)])
        y_rwkv, v_first = rwkv7(
            proj_rwkv, mu.reshape(1, -1),
            jnp.concatenate([w_lora[l], zero_lora], axis=0).astype(BF16),
            jnp.concatenate([zero_lora, a_lora[l]], axis=0).astype(BF16),
            g_lora[l].astype(BF16),
            _pad_rows(v_lora[l - 1], LANES).astype(BF16) if l > 0 else None,
            vecs, v_first)

        h2 = merge_branches(h2, y_conv.reshape(b * s, MIX_W), y_sb.reshape(b * s, MIX_W),
                            y_rwkv.reshape(b * s, MIX_W), proj_main.reshape(b * s, MAIN_COLS),
                            gate_b[l], branch_proj[l].astype(BF16), w_mix_out[l].astype(BF16))

        kv = norm_matmul(mem.reshape(b * mem_len, d), norm_mem[l], ca_wkv[l].astype(BF16), BF16)
        h = cross_attention(h2.reshape(b, s, d), kv.reshape(b, mem_len, 2 * d),
                            norm_ca[l].reshape(1, d), ca_wq[l].astype(BF16), ca_wo[l].astype(BF16))

        h = conv_ffn(h, norm_ffn[l].reshape(1, d), ffn_up[l].astype(BF16), ffn_conv_w[l],
                     ffn_conv_b[l].reshape(1, -1), ffn_down[l].astype(BF16),
                     norm_final.reshape(1, d), final_norm=(l == depth - 1))
    return h
```

```python
import functools
import math

import jax
import jax.numpy as jnp
from jax import lax
from jax.experimental import pallas as pl
from jax.experimental.pallas import tpu as pltpu

F32 = jnp.float32
BF16 = jnp.bfloat16

D_MODEL = 1024
MIX_W = 512
N_BRANCH = 3
SB_HEADS = 8
SB_HEAD_DIM = MIX_W // SB_HEADS
RWKV_HEADS = 8
RWKV_HEAD = MIX_W // RWKV_HEADS
DECAY_LORA = 64
ICLR_LORA = 64
GATE_LORA = 128
VRES_LORA = 32
DECAY_SCALE = math.exp(-0.5)
CA_HEADS = 4
CA_HEAD_DIM = D_MODEL // CA_HEADS
FFN_DIM = 2816
RMS_EPS = 1e-6
GN_EPS = 64e-5

MAIN_COLS = 3 * MIX_W + 3 * MIX_W + N_BRANCH * D_MODEL
RWKV_COLS = 3 * MIX_W + DECAY_LORA + ICLR_LORA + GATE_LORA
LANES = 128
RWKV_COLS_VRES = RWKV_COLS + LANES
VMEM_LIMIT = 48 * 1024 * 1024
FFN_VMEM_LIMIT = 56 * 1024 * 1024

SB_BLOCK = 128
SB_QBLOCKS = 2
LOG2E = math.log2(math.e)
SB_DEAD_LOG2 = 104.0 * LOG2E
RWKV_CHUNK = 64
RWKV_PAIR = 2 * RWKV_HEAD
RWKV_CHUNKS_PER_ITER = 4
MXU_WIDTH = 256


def _dot(a, b):
    return jnp.dot(a.astype(BF16), b.astype(BF16), preferred_element_type=F32)


def _dot_nt(a, b):
    return lax.dot_general(a.astype(BF16), b.astype(BF16), (((1,), (1,)), ((), ())),
                           preferred_element_type=F32)


def _head_sums(x, seg):
    xb = x.astype(BF16)
    n = seg.shape[0]
    return jnp.concatenate([jnp.dot(xb[:, i:i + n], seg, preferred_element_type=F32)
                            for i in range(0, x.shape[1], n)], axis=1)


def _rms(x, g):
    ms = jnp.mean(x * x, axis=-1, keepdims=True)
    return x * lax.rsqrt(ms + RMS_EPS) * g


def _shift_rows(u, prev8, n):
    rows = lax.broadcasted_iota(jnp.int32, u.shape, 0)
    r = pltpu.roll(u, n, 0)
    if n == 1:
        return jnp.where(rows == 0, prev8[7:8], r)
    return jnp.where(rows == 0, prev8[6:7], jnp.where(rows == 1, prev8[7:8], r))


def _pick(n, cands):
    for c in cands:
        if n % c == 0:
            return c
    raise ValueError(f"no tile for {n} in {cands}")


def _norm_mm_kernel(x_ref, g_ref, w_ref, o_ref, xn_ref):
    @pl.when(pl.program_id(1) == 0)
    def _():
        xn_ref[...] = _rms(x_ref[...], g_ref[...]).astype(BF16)

    o_ref[...] = jnp.dot(xn_ref[...], w_ref[...], preferred_element_type=F32).astype(o_ref.dtype)


def norm_matmul(x, gain, w, out_dtype):
    m, k = x.shape
    n = w.shape[1]
    tm = _pick(m, (1024, 512, 256, 128))
    tn = _pick(n, (512, 384, 256, 128))
    return pl.pallas_call(
        _norm_mm_kernel,
        grid=(m // tm, n // tn),
        in_specs=[pl.BlockSpec((tm, k), lambda i, j: (i, 0)),
                  pl.BlockSpec((1, k), lambda i, j: (0, 0)),
                  pl.BlockSpec((k, tn), lambda i, j: (0, j))],
        out_specs=pl.BlockSpec((tm, tn), lambda i, j: (i, j)),
        out_shape=jax.ShapeDtypeStruct((m, n), out_dtype),
        scratch_shapes=[pltpu.VMEM((tm, k), BF16)],
        compiler_params=pltpu.CompilerParams(
            dimension_semantics=("parallel", "arbitrary"), vmem_limit_bytes=VMEM_LIMIT),
        name="norm_matmul",
    )(x, gain.reshape(1, k), w)


def _in_proj_kernel(x_ref, g_ref, wm_ref, wr_ref, om_ref, or_ref):
    xn = _rms(x_ref[...], g_ref[...]).astype(BF16)
    for w_ref, o_ref in ((wm_ref, om_ref), (wr_ref, or_ref)):
        cols = w_ref.shape[1]
        for lo in range(0, cols, 2 * MXU_WIDTH):
            n = min(2 * MXU_WIDTH, cols - lo)
            o_ref[:, lo:lo + n] = jnp.dot(xn, w_ref[:, lo:lo + n],
                                          preferred_element_type=F32).astype(o_ref.dtype)


def input_projection(x, gain, w_main, w_rwkv):
    m, k = x.shape
    tm = _pick(m, (512, 256, 128))
    nm, nr = w_main.shape[1], w_rwkv.shape[1]
    gain = gain.reshape(1, k)
    full = lambda a: pl.BlockSpec(a.shape, lambda i: (0,) * a.ndim, pipeline_mode=pl.Buffered(1))
    row = lambda c: pl.BlockSpec((tm, c), lambda i: (i, 0))
    return pl.pallas_call(
        _in_proj_kernel,
        grid=(m // tm,),
        in_specs=[row(k), full(gain), full(w_main), full(w_rwkv)],
        out_specs=(row(nm), row(nr)),
        out_shape=(jax.ShapeDtypeStruct((m, nm), BF16), jax.ShapeDtypeStruct((m, nr), F32)),
        compiler_params=pltpu.CompilerParams(
            dimension_semantics=("parallel",), vmem_limit_bytes=FFN_VMEM_LIMIT),
        name="input_projection",
    )(x, gain, w_main, w_rwkv)


def _stack_pair(x, lane_lo):
    z = jnp.zeros_like(x)
    return jnp.concatenate([jnp.where(lane_lo, x, z), jnp.where(lane_lo, z, x)], axis=0)


def _sb_kernel(q_ref, k_ref, v_ref, tri_ref, o_ref, acc_ref, c_ref, ks_ref, vs_ref):
    tq = SB_BLOCK
    nq = SB_QBLOCKS
    npair = MIX_W // LANES
    zero_blk = ks_ref.shape[0] - npair
    units = [(a, p) for a in range(nq) for p in range(npair)]
    un = range(len(units))
    step = pl.program_id(1)
    lane_lo = lax.broadcasted_iota(jnp.int32, (tq, LANES), 1) < SB_HEAD_DIM
    row = lax.broadcasted_iota(jnp.int32, (tq, 2 * tq), 0)
    col = lax.broadcasted_iota(jnp.int32, (tq, 2 * tq), 1)
    causal = jnp.where(col < tq, col, col - tq) < row
    halves = [slice(0, tq), slice(tq, 2 * tq)]
    tri = tri_ref[...]
    sign = jnp.uint32(0x80000000)

    @pl.when(step == 0)
    def _():
        for p in range(npair):
            ks_ref[zero_blk + p] = jnp.zeros(ks_ref.shape[1:], BF16)
            vs_ref[zero_blk + p] = jnp.zeros(vs_ref.shape[1:], BF16)

    q = []
    for a, p in units:
        rs, ls = slice(a * tq, (a + 1) * tq), slice(p * LANES, (p + 1) * LANES)
        q.append(q_ref[rs, ls] * (SB_HEAD_DIM ** -0.5))
        ks_ref[(step * nq + a) * npair + p] = _stack_pair(k_ref[rs, ls], lane_lo)
        vs_ref[(step * nq + a) * npair + p] = _stack_pair(v_ref[rs, ls], lane_lo)

    def walk(i, diag):
        jb = [step * nq + a - i for a in range(nq)]
        blk = [jnp.where(jb[a] >= 0, jb[a] * npair, zero_blk) + p for a, p in units]
        ks = [ks_ref[blk[u]] for u in un]
        vs = [vs_ref[blk[u]] for u in un]
        z = [_dot_nt(q[u], ks[u]) * LOG2E for u in un]
        nabs = [pltpu.bitcast(pltpu.bitcast(z[u], jnp.uint32) | sign, F32) for u in un]
        sp = [jnp.maximum(z[u], 0.0) + jnp.log(1.0 + jnp.exp2(nabs[u])) * LOG2E for u in un]
        lsig = [z[u] - sp[u] for u in un]
        if diag:
            sp = [jnp.where(causal, sp[u], 0.0) for u in un]
        hi = [sp[u].astype(BF16) for u in un]
        lo = [(sp[u] - hi[u].astype(F32)).astype(BF16) for u in un]
        suf = [jnp.dot(jnp.concatenate([hi[u], lo[u]], axis=1), tri, preferred_element_type=F32)
               for u in un]
        c = [[c_ref[u, e] for e in range(2)] for u in un]
        att = [jnp.concatenate([jnp.exp2(lsig[u][:, hs] - suf[u][:, hs] - c[u][e])
                                for e, hs in enumerate(halves)], axis=1) for u in un]
        if diag:
            att = [jnp.where(causal, att[u], 0.0) for u in un]
        cmin = [None] * nq
        for u, (a, p) in enumerate(units):
            acc_ref[u] += jnp.dot(att[u].astype(BF16), vs[u], preferred_element_type=F32)
            for e, hs in enumerate(halves):
                cn = c[u][e] + jnp.sum(sp[u][:, hs], axis=1, keepdims=True)
                c_ref[u, e] = cn
                cmin[a] = cn if cmin[a] is None else jnp.minimum(cmin[a], cn)
        alive = [jnp.logical_and(jb[a] >= 1, jnp.min(cmin[a]) <= SB_DEAD_LOG2) for a in range(nq)]
        return functools.reduce(jnp.logical_or, alive)

    acc_ref[...] = jnp.zeros_like(acc_ref)
    c_ref[...] = jnp.zeros_like(c_ref)
    alive0 = walk(0, True)
    lax.while_loop(lambda s: s[1], lambda s: (s[0] + 1, walk(s[0], False)), (1, alive0))
    for a in range(nq):
        o_ref[a * tq:(a + 1) * tq, :] = jnp.concatenate(
            [acc_ref[a * npair + p] for p in range(npair)], axis=1).astype(o_ref.dtype)


def stick_breaking(proj_main):
    b, s, _ = proj_main.shape
    tq = SB_BLOCK
    pairs = MIX_W // LANES
    half = jnp.arange(2 * tq) // tq
    pos = jnp.arange(2 * tq)
    same = half[:, None] == half[None, :]
    later = same & (pos[:, None] > pos[None, :])
    tri = jnp.concatenate([later, later], axis=0).astype(BF16)
    nq = SB_QBLOCKS
    blk = lambda c: pl.BlockSpec((None, nq * tq, MIX_W), lambda i, t, c=c: (i, t, c))
    kv_blocks = (s // tq + 1) * pairs
    return pl.pallas_call(
        _sb_kernel,
        grid=(b, s // (nq * tq)),
        in_specs=[blk(3), blk(4), blk(5), pl.BlockSpec(tri.shape, lambda i, t: (0, 0))],
        out_specs=blk(0),
        out_shape=jax.ShapeDtypeStruct((b, s, MIX_W), BF16),
        scratch_shapes=[pltpu.VMEM((nq * pairs, tq, LANES), F32),
                        pltpu.VMEM((nq * pairs, 2, tq, tq), F32),
                        pltpu.VMEM((kv_blocks, 2 * tq, LANES), BF16),
                        pltpu.VMEM((kv_blocks, 2 * tq, LANES), BF16)],
        compiler_params=pltpu.CompilerParams(
            dimension_semantics=("parallel", "arbitrary"), vmem_limit_bytes=VMEM_LIMIT),
        name="stick_breaking",
    )(proj_main, proj_main, proj_main, tri)


def _rwkv_kernel(*refs, vres):
    if vres:
        (p_ref, mu_ref, wl_ref, al_ref, gl_ref, vl_ref, vec_ref, seg_ref, vfirst_ref,
         y_ref, carry_ref, r_s, lw_s, k_s, v_s, kk_s, b_s, o_s, st_s) = refs
    else:
        (p_ref, mu_ref, wl_ref, al_ref, gl_ref, vec_ref, seg_ref,
         y_ref, vfirst_ref, carry_ref, r_s, lw_s, k_s, v_s, kk_s, b_s, o_s, st_s) = refs
    ts = p_ref.shape[0]
    ck = RWKV_CHUNK
    w_ = MIX_W

    @pl.when(pl.program_id(1) == 0)
    def _():
        carry_ref[...] = jnp.zeros_like(carry_ref)
        st_s[...] = jnp.zeros_like(st_s)

    p = p_ref[...]
    prev = _shift_rows(p, carry_ref[...], 1)
    carry_ref[...] = p[ts - 8:]
    p = p + mu_ref[...] * (prev - p)
    w0, a0, k_k, k_a, r_k, lnx_w, lnx_b, v0 = (vec_ref[i:i + 1] for i in range(8))
    seg = seg_ref[...]
    r = p[:, :w_]
    k = p[:, w_:2 * w_]
    v = p[:, 2 * w_:3 * w_]
    x_wa = p[:, 3 * w_:3 * w_ + LANES]
    x_g = p[:, 3 * w_ + LANES:3 * w_ + 2 * LANES]
    lw = -DECAY_SCALE * jax.nn.sigmoid(w0 + _dot(jnp.tanh(x_wa), wl_ref[...]))
    a = jax.nn.sigmoid(a0 + _dot(x_wa, al_ref[...]))
    if vres:
        x_v = p[:, 3 * w_ + 2 * LANES:3 * w_ + 3 * LANES]
        v = v + (vfirst_ref[...] - v) * jax.nn.sigmoid(v0 + _dot(x_v, vl_ref[...]))
    else:
        vfirst_ref[...] = v
    g = _dot(jax.nn.sigmoid(x_g), gl_ref[...])
    kk = k * k_k
    kk = kk / jnp.maximum(jnp.sqrt(_head_sums(kk * kk, seg)), 1e-12)
    k = k * (1.0 + (a - 1.0) * k_a)
    r_s[...] = r
    lw_s[...] = lw
    k_s[...] = k
    v_s[...] = v
    kk_s[...] = kk
    b_s[...] = kk * a

    rc = lax.broadcasted_iota(jnp.int32, (ck, ck), 0)
    cc = lax.broadcasted_iota(jnp.int32, (ck, ck), 1)
    tril = jnp.where(cc <= rc, 1.0, 0.0).astype(BF16)
    r2 = lax.broadcasted_iota(jnp.int32, (2 * ck, 2 * ck), 0)
    c2 = lax.broadcasted_iota(jnp.int32, (2 * ck, 2 * ck), 1)
    same = (r2 // ck) == (c2 // ck)
    strict = jnp.logical_and(same, c2 < r2)
    incl = jnp.logical_and(same, c2 <= r2)
    eye = jnp.where(r2 == c2, 1.0, 0.0)
    lane_lo = lax.broadcasted_iota(jnp.int32, (ck, RWKV_PAIR), 1) < RWKV_HEAD

    npair = w_ // RWKV_PAIR
    nch = RWKV_CHUNKS_PER_ITER
    sl = [slice(p * RWKV_PAIR, (p + 1) * RWKV_PAIR) for p in range(npair)]

    def chunks(ci, _):
        units = [(j, p) for j in range(nch) for p in range(npair)]
        a_m, b_m, k_m, r_m, v_m, nb_end, k_end, dec, rows = [], [], [], [], [], [], [], [], []
        for j in range(nch):
            rws = pl.ds(pl.multiple_of((ci * nch + j) * ck, ck), ck)
            rows.append(rws)
            lw_c = lw_s[rws, :]
            lw_hi = lw_c.astype(BF16)
            lw_lo = (lw_c - lw_hi.astype(F32)).astype(BF16)
            cum = (jnp.dot(tril, lw_hi, preferred_element_type=F32)
                   + jnp.dot(tril, lw_lo, preferred_element_type=F32))
            e_neg = jnp.exp(-cum)
            total = cum[ck - 1:ck]
            e_end = jnp.exp(total - cum)
            e_tot = jnp.exp(total)
            a_all = kk_s[rws, :] * jnp.exp(cum - lw_c)
            b_all = b_s[rws, :]
            k_all = k_s[rws, :]
            r_all = r_s[rws, :] * jnp.exp(cum)
            v_all = v_s[rws, :]
            stk = lambda x: [_stack_pair(x[:, ls], lane_lo).astype(BF16) for ls in sl]
            a_m += stk(a_all)
            b_m += stk(b_all * e_neg)
            k_m += stk(k_all * e_neg)
            r_m += stk(r_all)
            v_m += stk(v_all)
            nb_end += stk(b_all * -e_end)
            k_end += stk(k_all * e_end)
            dec += [jnp.broadcast_to(e_tot[:, ls], (2 * ck, RWKV_PAIR)).T for ls in sl]
        un = range(len(units))
        n2 = 2 * ck
        mm = [_dot_nt(jnp.concatenate([a_m[i], r_m[i]], axis=0),
                      jnp.concatenate([b_m[i], k_m[i]], axis=0)) for i in un]
        l_ab = [jnp.where(strict, mm[i][:n2, :n2], 0.0) for i in un]
        l_ak = [jnp.where(strict, mm[i][:n2, n2:], 0.0) for i in un]
        m_rb = [jnp.where(incl, mm[i][n2:, :n2], 0.0).astype(BF16) for i in un]
        m_rk = [jnp.where(incl, mm[i][n2:, n2:], 0.0).astype(BF16) for i in un]
        t_inv = [eye - l_ab[i] for i in un]
        pw = [l_ab[i].astype(BF16) for i in un]
        for _i in range(5):
            pw = [_dot(pw[i], pw[i]).astype(BF16) for i in un]
            t_inv = [t_inv[i] + _dot(t_inv[i], pw[i]) for i in un]
        t_inv = [t_inv[i].astype(BF16) for i in un]
        lv = [_dot(l_ak[i], v_m[i]).astype(BF16) for i in un]
        w12 = [_dot(t_inv[i], jnp.concatenate([a_m[i], lv[i]], axis=1)) for i in un]
        w1 = [w12[i][:, :n2].astype(BF16) for i in un]
        w2 = [w12[i][:, n2:] for i in un]
        rk = [jnp.concatenate([r_m[i], m_rk[i]], axis=1) for i in un]
        kbt = [jnp.concatenate([k_end[i].astype(F32).T.astype(BF16),
                                nb_end[i].astype(F32).T.astype(BF16)], axis=1) for i in un]
        st = [st_s[p] for p in range(npair)]
        for j in range(nch):
            ids = [j * npair + p for p in range(npair)]
            st_b = [st[p].astype(BF16) for p in range(npair)]
            u = [_dot(w1[i], st_b[p]) + w2[i] for p, i in enumerate(ids)]
            ub = [u[p].astype(BF16) for p in range(npair)]
            o = [_dot(rk[i], jnp.concatenate([st_b[p], v_m[i]], axis=0)) - _dot(m_rb[i], ub[p])
                 for p, i in enumerate(ids)]
            st = [dec[i] * st[p] + _dot(kbt[i], jnp.concatenate([v_m[i], ub[p]], axis=0))
                  for p, i in enumerate(ids)]
            for p in range(npair):
                o_s[rows[j], sl[p]] = o[p][:ck] + o[p][ck:]
        for p in range(npair):
            st_s[p] = st[p]
        return 0

    lax.fori_loop(0, ts // (ck * nch), chunks, 0)

    o = o_s[...]
    inv_n = 1.0 / RWKV_HEAD
    mean = _head_sums(o, seg) * inv_n
    d = o - mean
    var = _head_sums(d * d, seg) * inv_n
    gn = d * lax.rsqrt(var + GN_EPS) * lnx_w + lnx_b
    bonus = _head_sums(r * k * r_k, seg) * v
    y_ref[...] = ((gn + bonus) * g).astype(y_ref.dtype)


def rwkv7(p, mu, w_lora, a_lora, g_lora, v_lora, vecs, v_first):
    b, s, cols = p.shape
    vres = v_first is not None
    ts = _pick(s, (512, 256, 128, 64))
    head = jnp.arange(MXU_WIDTH) // RWKV_HEAD
    seg = (head[:, None] == head[None, :]).astype(BF16)
    tile = lambda c: pl.BlockSpec((None, ts, c), lambda i, t: (i, t, 0))
    full = lambda a: pl.BlockSpec(a.shape, lambda i, t: (0,) * a.ndim)
    ins = [p, mu, w_lora, a_lora, g_lora] + ([v_lora] if vres else []) + [vecs, seg]
    in_specs = [tile(cols)] + [full(a) for a in ins[1:]]
    y_shape = jax.ShapeDtypeStruct((b, s, MIX_W), BF16)
    if vres:
        ins.append(v_first)
        in_specs.append(tile(MIX_W))
        out_shape, out_specs = y_shape, tile(MIX_W)
    else:
        out_shape = (y_shape, jax.ShapeDtypeStruct((b, s, MIX_W), F32))
        out_specs = (tile(MIX_W), tile(MIX_W))
    pairs = MIX_W // RWKV_PAIR
    scratch = ([pltpu.VMEM((8, cols), F32)] + [pltpu.VMEM((ts, MIX_W), F32)] * 7
               + [pltpu.VMEM((pairs, RWKV_PAIR, RWKV_PAIR), F32)])
    out = pl.pallas_call(
        functools.partial(_rwkv_kernel, vres=vres),
        grid=(b, s // ts),
        in_specs=in_specs, out_specs=out_specs, out_shape=out_shape,
        scratch_shapes=scratch,
        compiler_params=pltpu.CompilerParams(
            dimension_semantics=("parallel", "arbitrary"), vmem_limit_bytes=VMEM_LIMIT),
        name="rwkv7",
    )(*ins)
    return (out, v_first) if vres else out


def _mix_ca_kernel(cb_ref, cc_ref, cx_ref, cw_ref, ys_ref, yr_ref, gate_ref, gb_ref, bp_ref, wmo_ref,
                   h_ref, g_ref, wq_ref, k_ref, v_ref, wo_ref, o_ref, carry_ref):
    @pl.when(pl.program_id(1) == 0)
    def _():
        carry_ref[...] = jnp.zeros_like(carry_ref)

    u = cc_ref[...].astype(F32) * cx_ref[...].astype(F32)
    prev8 = carry_ref[...]
    conv = (cw_ref[2:3] * u + cw_ref[1:2] * _shift_rows(u, prev8, 1)
            + cw_ref[0:1] * _shift_rows(u, prev8, 2))
    carry_ref[...] = u[u.shape[0] - 8:]
    y_conv = (cb_ref[...].astype(F32) * conv).astype(BF16)

    merged = None
    for n, y in enumerate((y_conv, ys_ref[...], yr_ref[...])):
        branch = jnp.dot(y, bp_ref[n], preferred_element_type=F32)
        gate = jax.nn.sigmoid(gate_ref[:, n * D_MODEL:(n + 1) * D_MODEL].astype(F32) + gb_ref[n:n + 1])
        merged = gate * branch if merged is None else merged + gate * branch
    h = h_ref[...] + _dot(merged, wmo_ref[...])

    q = _dot(_rms(h, g_ref[...]), wq_ref[...]).astype(BF16)
    heads = [slice(hd * CA_HEAD_DIM, (hd + 1) * CA_HEAD_DIM) for hd in range(CA_HEADS)]
    sc = [_dot_nt(q[:, sl], k_ref[:, sl]) * (CA_HEAD_DIM ** -0.5) for sl in heads]
    e = [jnp.exp(x - jnp.max(x, axis=-1, keepdims=True)) for x in sc]
    probs = [x / jnp.sum(x, axis=-1, keepdims=True) for x in e]
    outs = [_dot(pr, v_ref[:, sl]) for pr, sl in zip(probs, heads)]
    o_ref[...] = h + _dot(jnp.concatenate(outs, axis=1), wo_ref[...])


def mix_cross_attention(h, proj_main, y_sb, y_rwkv, conv_w, gate_b, branch_proj, w_mix_out,
                        kv, gain_ca, wq, wo):
    b, s, d = h.shape
    mem = kv.shape[1]
    tm = _pick(s, (512, 256, 128))
    full = lambda a: pl.BlockSpec(a.shape, lambda i, t: (0,) * a.ndim, pipeline_mode=pl.Buffered(1))
    tile = lambda c, j=0: pl.BlockSpec((None, tm, c), lambda i, t, j=j: (i, t, j))
    return pl.pallas_call(
        _mix_ca_kernel,
        grid=(b, s // tm),
        in_specs=[tile(MIX_W, 0), tile(MIX_W, 1), tile(MIX_W, 2), full(conv_w),
                  tile(MIX_W), tile(MIX_W), tile(N_BRANCH * D_MODEL, 1),
                  full(gate_b), full(branch_proj), full(w_mix_out),
                  tile(d), full(gain_ca), full(wq),
                  pl.BlockSpec((None, mem, d), lambda i, t: (i, 0, 0)),
                  pl.BlockSpec((None, mem, d), lambda i, t: (i, 0, 1)),
                  full(wo)],
        out_specs=tile(d),
        out_shape=jax.ShapeDtypeStruct((b, s, d), F32),
        scratch_shapes=[pltpu.VMEM((8, MIX_W), F32)],
        compiler_params=pltpu.CompilerParams(
            dimension_semantics=("parallel", "arbitrary"), vmem_limit_bytes=VMEM_LIMIT),
        name="mix_cross_attention",
    )(proj_main, proj_main, proj_main, conv_w, y_sb, y_rwkv, proj_main, gate_b, branch_proj,
      w_mix_out, h, gain_ca, wq, kv, kv, wo)


def _conv_ffn_kernel(h_ref, g_ref, wup_ref, cw_ref, cb_ref, wdn_ref, gf_ref, o_ref, carry_ref,
                     act_ref, *, fc, final_norm):
    tm = h_ref.shape[0]
    nc = FFN_DIM // fc

    @pl.when(pl.program_id(1) == 0)
    def _():
        carry_ref[...] = jnp.zeros_like(carry_ref)

    h = h_ref[...]
    hn = _rms(h, g_ref[...]).astype(BF16)
    for c in range(nc):
        halves = []
        for part in range(2):
            lo = part * FFN_DIM + c * fc
            u = jnp.dot(hn, wup_ref[:, lo:lo + fc], preferred_element_type=F32)
            prev8 = carry_ref[part * nc + c]
            cw = cw_ref[:, lo:lo + fc]
            halves.append(cw[2:3] * u + cw[1:2] * _shift_rows(u, prev8, 1)
                          + cw[0:1] * _shift_rows(u, prev8, 2) + cb_ref[:, lo:lo + fc])
            carry_ref[part * nc + c] = u[tm - 8:]
        act_ref[:, c * fc:(c + 1) * fc] = (jax.nn.silu(halves[0]) * halves[1]).astype(BF16)
    out = h + jnp.dot(act_ref[...], wdn_ref[...], preferred_element_type=F32)
    o_ref[...] = _rms(out, gf_ref[...]) if final_norm else out


def conv_ffn(h, gain, w_up, conv_w, conv_b, w_down, gain_final, final_norm):
    b, s, d = h.shape
    tm = _pick(s, (512, 256, 128))
    fc = 256
    full = lambda a: pl.BlockSpec(a.shape, lambda i, t: (0,) * a.ndim, pipeline_mode=pl.Buffered(1))
    return pl.pallas_call(
        functools.partial(_conv_ffn_kernel, fc=fc, final_norm=final_norm),
        grid=(b, s // tm),
        in_specs=[pl.BlockSpec((None, tm, d), lambda i, t: (i, t, 0)),
                  full(gain), full(w_up), full(conv_w), full(conv_b), full(w_down), full(gain_final)],
        out_specs=pl.BlockSpec((None, tm, d), lambda i, t: (i, t, 0)),
        out_shape=jax.ShapeDtypeStruct((b, s, d), F32),
        scratch_shapes=[pltpu.VMEM((2 * FFN_DIM // fc, 8, fc), F32), pltpu.VMEM((tm, FFN_DIM), BF16)],
        compiler_params=pltpu.CompilerParams(
            dimension_semantics=("parallel", "arbitrary"), vmem_limit_bytes=FFN_VMEM_LIMIT),
        name="conv_ffn",
    )(h, gain, w_up, conv_w, conv_b, w_down, gain_final)


def _pad_rows(w, rows):
    return jnp.concatenate([w, jnp.zeros((rows - w.shape[0],) + w.shape[1:], w.dtype)], axis=0)


def kernel(x, mem, norm_mix, w_comb, conv_w, mu_rwkv, w0, w_lora, a0, a_lora, g_lora, k_k, k_a, r_k, lnx_w, lnx_b, w_vres, mu_vres, v0, v_lora, branch_proj, gate_b, w_mix_out, norm_ca, norm_mem, ca_wq, ca_wkv, ca_wo, norm_ffn, ffn_up, ffn_conv_w, ffn_conv_b, ffn_down, norm_final):
    b, s, d = x.shape
    depth = w_comb.shape[0]
    mem_len = mem.shape[1]
    h = x
    v_first = None
    zero_lora = jnp.zeros((DECAY_LORA, MIX_W), F32)
    for l in range(depth):
        h2 = h.reshape(b * s, d)
        w_main = w_comb[l, :, :MAIN_COLS].astype(BF16)
        w_rwkv = w_comb[l, :, MAIN_COLS:]
        mu = mu_rwkv[l]
        if l > 0:
            pad = LANES - VRES_LORA
            w_rwkv = jnp.concatenate([w_rwkv, w_vres[l - 1], jnp.zeros((d, pad), F32)], axis=1)
            mu = jnp.concatenate([mu, mu_vres[l - 1], jnp.zeros((pad,), F32)])
        proj_main, proj_rwkv = input_projection(h2, norm_mix[l], w_main, w_rwkv.astype(BF16))
        proj_main = proj_main.reshape(b, s, MAIN_COLS)
        proj_rwkv = proj_rwkv.reshape(b, s, -1)

        y_sb = stick_breaking(proj_main)
        vecs = jnp.stack([w0[l], a0[l], k_k[l], k_a[l], r_k[l].reshape(-1), lnx_w[l], lnx_b[l],
                          v0[l - 1] if l > 0 else jnp.zeros((MIX_W,), F32)])
        y_rwkv, v_first = rwkv7(
            proj_rwkv, mu.reshape(1, -1),
            jnp.concatenate([w_lora[l], zero_lora], axis=0).astype(BF16),
            jnp.concatenate([zero_lora, a_lora[l]], axis=0).astype(BF16),
            g_lora[l].astype(BF16),
            _pad_rows(v_lora[l - 1], LANES).astype(BF16) if l > 0 else None,
            vecs, v_first)

        kv = norm_matmul(mem.reshape(b * mem_len, d), norm_mem[l], ca_wkv[l].astype(BF16), BF16)
        h = mix_cross_attention(h, proj_main, y_sb, y_rwkv, conv_w[l], gate_b[l],
                                branch_proj[l].astype(BF16), w_mix_out[l].astype(BF16),
                                kv.reshape(b, mem_len, 2 * d), norm_ca[l].reshape(1, d),
                                ca_wq[l].astype(BF16), ca_wo[l].astype(BF16))

        h = conv_ffn(h, norm_ffn[l].reshape(1, d), ffn_up[l].astype(BF16), ffn_conv_w[l],
                     ffn_conv_b[l].reshape(1, -1), ffn_down[l].astype(BF16),
                     norm_final.reshape(1, d), final_norm=(l == depth - 1))
    return h
```

```python
import functools
import math

import jax
import jax.numpy as jnp
from jax import lax
from jax.experimental import pallas as pl
from jax.experimental.pallas import tpu as pltpu

F32 = jnp.float32
BF16 = jnp.bfloat16

D_MODEL = 1024
MIX_W = 512
N_BRANCH = 3
SB_HEADS = 8
SB_HEAD_DIM = MIX_W // SB_HEADS
RWKV_HEADS = 8
RWKV_HEAD = MIX_W // RWKV_HEADS
DECAY_LORA = 64
ICLR_LORA = 64
GATE_LORA = 128
VRES_LORA = 32
DECAY_SCALE = math.exp(-0.5)
CA_HEADS = 4
CA_HEAD_DIM = D_MODEL // CA_HEADS
FFN_DIM = 2816
RMS_EPS = 1e-6
GN_EPS = 64e-5

MAIN_COLS = 3 * MIX_W + 3 * MIX_W + N_BRANCH * D_MODEL
RWKV_COLS = 3 * MIX_W + DECAY_LORA + ICLR_LORA + GATE_LORA
LANES = 128
RWKV_COLS_VRES = RWKV_COLS + LANES
VMEM_LIMIT = 48 * 1024 * 1024
FFN_VMEM_LIMIT = 56 * 1024 * 1024

SB_BLOCK = 128
SB_QBLOCKS = 2
LOG2E = math.log2(math.e)
SB_DEAD_LOG2 = 104.0 * LOG2E
RWKV_CHUNK = 64
RWKV_PAIR = 2 * RWKV_HEAD
RWKV_CHUNKS_PER_ITER = 4
RWKV_STAGES_PER_UPDATE = 1
MXU_WIDTH = 256


def _dot(a, b):
    return jnp.dot(a.astype(BF16), b.astype(BF16), preferred_element_type=F32)


def _dot_nt(a, b):
    return lax.dot_general(a.astype(BF16), b.astype(BF16), (((1,), (1,)), ((), ())),
                           preferred_element_type=F32)


def _head_sums(x, seg):
    xb = x.astype(BF16)
    n = seg.shape[0]
    return jnp.concatenate([jnp.dot(xb[:, i:i + n], seg, preferred_element_type=F32)
                            for i in range(0, x.shape[1], n)], axis=1)


def _rms(x, g):
    ms = jnp.mean(x * x, axis=-1, keepdims=True)
    return x * lax.rsqrt(ms + RMS_EPS) * g


def _shift_rows(u, prev8, n):
    rows = lax.broadcasted_iota(jnp.int32, u.shape, 0)
    r = pltpu.roll(u, n, 0)
    if n == 1:
        return jnp.where(rows == 0, prev8[7:8], r)
    return jnp.where(rows == 0, prev8[6:7], jnp.where(rows == 1, prev8[7:8], r))


def _pick(n, cands):
    for c in cands:
        if n % c == 0:
            return c
    raise ValueError(f"no tile for {n} in {cands}")


def _norm_mm_kernel(x_ref, g_ref, w_ref, o_ref, xn_ref):
    @pl.when(pl.program_id(1) == 0)
    def _():
        xn_ref[...] = _rms(x_ref[...], g_ref[...]).astype(BF16)

    o_ref[...] = jnp.dot(xn_ref[...], w_ref[...], preferred_element_type=F32).astype(o_ref.dtype)


def norm_matmul(x, gain, w, out_dtype):
    m, k = x.shape
    n = w.shape[1]
    tm = _pick(m, (1024, 512, 256, 128))
    tn = _pick(n, (512, 384, 256, 128))
    return pl.pallas_call(
        _norm_mm_kernel,
        grid=(m // tm, n // tn),
        in_specs=[pl.BlockSpec((tm, k), lambda i, j: (i, 0)),
                  pl.BlockSpec((1, k), lambda i, j: (0, 0)),
                  pl.BlockSpec((k, tn), lambda i, j: (0, j))],
        out_specs=pl.BlockSpec((tm, tn), lambda i, j: (i, j)),
        out_shape=jax.ShapeDtypeStruct((m, n), out_dtype),
        scratch_shapes=[pltpu.VMEM((tm, k), BF16)],
        compiler_params=pltpu.CompilerParams(
            dimension_semantics=("parallel", "arbitrary"), vmem_limit_bytes=VMEM_LIMIT),
        name="norm_matmul",
    )(x, gain.reshape(1, k), w)


def _in_proj_kernel(x_ref, g_ref, wm_ref, wr_ref, om_ref, or_ref):
    xn = _rms(x_ref[...], g_ref[...]).astype(BF16)
    for w_ref, o_ref in ((wm_ref, om_ref), (wr_ref, or_ref)):
        cols = w_ref.shape[1]
        for lo in range(0, cols, 2 * MXU_WIDTH):
            n = min(2 * MXU_WIDTH, cols - lo)
            o_ref[:, lo:lo + n] = jnp.dot(xn, w_ref[:, lo:lo + n],
                                          preferred_element_type=F32).astype(o_ref.dtype)


def input_projection(x, gain, w_main, w_rwkv):
    m, k = x.shape
    tm = _pick(m, (512, 256, 128))
    nm, nr = w_main.shape[1], w_rwkv.shape[1]
    gain = gain.reshape(1, k)
    full = lambda a: pl.BlockSpec(a.shape, lambda i: (0,) * a.ndim, pipeline_mode=pl.Buffered(1))
    row = lambda c: pl.BlockSpec((tm, c), lambda i: (i, 0))
    return pl.pallas_call(
        _in_proj_kernel,
        grid=(m // tm,),
        in_specs=[row(k), full(gain), full(w_main), full(w_rwkv)],
        out_specs=(row(nm), row(nr)),
        out_shape=(jax.ShapeDtypeStruct((m, nm), BF16), jax.ShapeDtypeStruct((m, nr), F32)),
        compiler_params=pltpu.CompilerParams(
            dimension_semantics=("parallel",), vmem_limit_bytes=FFN_VMEM_LIMIT),
        name="input_projection",
    )(x, gain, w_main, w_rwkv)


def _stack_pair(x, lane_lo):
    z = jnp.zeros_like(x)
    return jnp.concatenate([jnp.where(lane_lo, x, z), jnp.where(lane_lo, z, x)], axis=0)


def _sb_kernel(q_ref, k_ref, v_ref, tri_ref, o_ref, acc_ref, c_ref, ks_ref, vs_ref):
    tq = SB_BLOCK
    nq = SB_QBLOCKS
    npair = MIX_W // LANES
    zero_blk = ks_ref.shape[0] - npair
    units = [(a, p) for a in range(nq) for p in range(npair)]
    un = range(len(units))
    step = pl.program_id(1)
    lane_lo = lax.broadcasted_iota(jnp.int32, (tq, LANES), 1) < SB_HEAD_DIM
    row = lax.broadcasted_iota(jnp.int32, (tq, 2 * tq), 0)
    col = lax.broadcasted_iota(jnp.int32, (tq, 2 * tq), 1)
    causal = jnp.where(col < tq, col, col - tq) < row
    halves = [slice(0, tq), slice(tq, 2 * tq)]
    tri = tri_ref[...]
    sign = jnp.uint32(0x80000000)

    @pl.when(step == 0)
    def _():
        for p in range(npair):
            ks_ref[zero_blk + p] = jnp.zeros(ks_ref.shape[1:], BF16)
            vs_ref[zero_blk + p] = jnp.zeros(vs_ref.shape[1:], BF16)

    q = []
    for a, p in units:
        rs, ls = slice(a * tq, (a + 1) * tq), slice(p * LANES, (p + 1) * LANES)
        q.append(q_ref[rs, ls] * (SB_HEAD_DIM ** -0.5))
        ks_ref[(step * nq + a) * npair + p] = _stack_pair(k_ref[rs, ls], lane_lo)
        vs_ref[(step * nq + a) * npair + p] = _stack_pair(v_ref[rs, ls], lane_lo)

    def walk(i, diag):
        jb = [step * nq + a - i for a in range(nq)]
        blk = [jnp.where(jb[a] >= 0, jb[a] * npair, zero_blk) + p for a, p in units]
        ks = [ks_ref[blk[u]] for u in un]
        vs = [vs_ref[blk[u]] for u in un]
        z = [_dot_nt(q[u], ks[u]) * LOG2E for u in un]
        nabs = [pltpu.bitcast(pltpu.bitcast(z[u], jnp.uint32) | sign, F32) for u in un]
        sp = [jnp.maximum(z[u], 0.0) + jnp.log(1.0 + jnp.exp2(nabs[u])) * LOG2E for u in un]
        lsig = [z[u] - sp[u] for u in un]
        if diag:
            sp = [jnp.where(causal, sp[u], 0.0) for u in un]
        hi = [sp[u].astype(BF16) for u in un]
        lo = [(sp[u] - hi[u].astype(F32)).astype(BF16) for u in un]
        suf = [jnp.dot(jnp.concatenate([hi[u], lo[u]], axis=1), tri, preferred_element_type=F32)
               for u in un]
        c = [[c_ref[u, e] for e in range(2)] for u in un]
        att = [jnp.concatenate([jnp.exp2(lsig[u][:, hs] - suf[u][:, hs] - c[u][e])
                                for e, hs in enumerate(halves)], axis=1) for u in un]
        if diag:
            att = [jnp.where(causal, att[u], 0.0) for u in un]
        cmin = [None] * nq
        for u, (a, p) in enumerate(units):
            acc_ref[u] += jnp.dot(att[u].astype(BF16), vs[u], preferred_element_type=F32)
            for e, hs in enumerate(halves):
                cn = c[u][e] + jnp.sum(sp[u][:, hs], axis=1, keepdims=True)
                c_ref[u, e] = cn
                cmin[a] = cn if cmin[a] is None else jnp.minimum(cmin[a], cn)
        alive = [jnp.logical_and(jb[a] >= 1, jnp.min(cmin[a]) <= SB_DEAD_LOG2) for a in range(nq)]
        return functools.reduce(jnp.logical_or, alive)

    acc_ref[...] = jnp.zeros_like(acc_ref)
    c_ref[...] = jnp.zeros_like(c_ref)
    alive0 = walk(0, True)
    lax.while_loop(lambda s: s[1], lambda s: (s[0] + 1, walk(s[0], False)), (1, alive0))
    for a in range(nq):
        o_ref[a * tq:(a + 1) * tq, :] = jnp.concatenate(
            [acc_ref[a * npair + p] for p in range(npair)], axis=1).astype(o_ref.dtype)


def stick_breaking(proj_main):
    b, s, _ = proj_main.shape
    tq = SB_BLOCK
    pairs = MIX_W // LANES
    half = jnp.arange(2 * tq) // tq
    pos = jnp.arange(2 * tq)
    same = half[:, None] == half[None, :]
    later = same & (pos[:, None] > pos[None, :])
    tri = jnp.concatenate([later, later], axis=0).astype(BF16)
    nq = SB_QBLOCKS
    blk = lambda c: pl.BlockSpec((None, nq * tq, MIX_W), lambda i, t, c=c: (i, t, c))
    kv_blocks = (s // tq + 1) * pairs
    return pl.pallas_call(
        _sb_kernel,
        grid=(b, s // (nq * tq)),
        in_specs=[blk(3), blk(4), blk(5), pl.BlockSpec(tri.shape, lambda i, t: (0, 0))],
        out_specs=blk(0),
        out_shape=jax.ShapeDtypeStruct((b, s, MIX_W), BF16),
        scratch_shapes=[pltpu.VMEM((nq * pairs, tq, LANES), F32),
                        pltpu.VMEM((nq * pairs, 2, tq, tq), F32),
                        pltpu.VMEM((kv_blocks, 2 * tq, LANES), BF16),
                        pltpu.VMEM((kv_blocks, 2 * tq, LANES), BF16)],
        compiler_params=pltpu.CompilerParams(
            dimension_semantics=("parallel", "arbitrary"), vmem_limit_bytes=VMEM_LIMIT),
        name="stick_breaking",
    )(proj_main, proj_main, proj_main, tri)


def _rwkv_kernel(*refs, vres):
    if vres:
        (p_ref, mu_ref, wl_ref, al_ref, gl_ref, vl_ref, vec_ref, seg_ref, vfirst_ref,
         y_ref, carry_ref, r_s, lw_s, k_s, v_s, kk_s, b_s, o_s, st_s) = refs
    else:
        (p_ref, mu_ref, wl_ref, al_ref, gl_ref, vec_ref, seg_ref,
         y_ref, vfirst_ref, carry_ref, r_s, lw_s, k_s, v_s, kk_s, b_s, o_s, st_s) = refs
    ts = p_ref.shape[0]
    ck = RWKV_CHUNK
    w_ = MIX_W

    @pl.when(pl.program_id(1) == 0)
    def _():
        carry_ref[...] = jnp.zeros_like(carry_ref)
        st_s[...] = jnp.zeros_like(st_s)

    p = p_ref[...]
    prev = _shift_rows(p, carry_ref[...], 1)
    carry_ref[...] = p[ts - 8:]
    p = p + mu_ref[...] * (prev - p)
    w0, a0, k_k, k_a, r_k, lnx_w, lnx_b, v0 = (vec_ref[i:i + 1] for i in range(8))
    seg = seg_ref[...]
    r = p[:, :w_]
    k = p[:, w_:2 * w_]
    v = p[:, 2 * w_:3 * w_]
    x_wa = p[:, 3 * w_:3 * w_ + LANES]
    x_g = p[:, 3 * w_ + LANES:3 * w_ + 2 * LANES]
    lw = -DECAY_SCALE * jax.nn.sigmoid(w0 + _dot(jnp.tanh(x_wa), wl_ref[...]))
    a = jax.nn.sigmoid(a0 + _dot(x_wa, al_ref[...]))
    if vres:
        x_v = p[:, 3 * w_ + 2 * LANES:3 * w_ + 3 * LANES]
        v = v + (vfirst_ref[...] - v) * jax.nn.sigmoid(v0 + _dot(x_v, vl_ref[...]))
    else:
        vfirst_ref[...] = v
    g = _dot(jax.nn.sigmoid(x_g), gl_ref[...])
    kk = k * k_k
    kk = kk / jnp.maximum(jnp.sqrt(_head_sums(kk * kk, seg)), 1e-12)
    k = k * (1.0 + (a - 1.0) * k_a)
    r_s[...] = r
    lw_s[...] = lw
    k_s[...] = k
    v_s[...] = v
    kk_s[...] = kk
    b_s[...] = kk * a

    rc = lax.broadcasted_iota(jnp.int32, (ck, ck), 0)
    cc = lax.broadcasted_iota(jnp.int32, (ck, ck), 1)
    tril = jnp.where(cc <= rc, 1.0, 0.0).astype(BF16)
    r2 = lax.broadcasted_iota(jnp.int32, (2 * ck, 2 * ck), 0)
    c2 = lax.broadcasted_iota(jnp.int32, (2 * ck, 2 * ck), 1)
    same = (r2 // ck) == (c2 // ck)
    strict = jnp.logical_and(same, c2 < r2)
    incl = jnp.logical_and(same, c2 <= r2)
    hb = ck // 2
    same_h = (r2 // hb) == (c2 // hb)
    strict_h = jnp.logical_and(same_h, c2 < r2)
    strict_o = jnp.logical_and(strict, jnp.logical_not(same_h))
    rh = lax.broadcasted_iota(jnp.int32, (hb, 2 * ck), 0)
    ch = lax.broadcasted_iota(jnp.int32, (hb, 2 * ck), 1)
    eye_c = jnp.where(ch % hb == rh, 1.0, 0.0)

    def compress(x):
        return functools.reduce(lambda a, b: a + b, [x[i:i + hb] for i in range(0, 2 * ck, hb)])

    def expand(xc):
        return jnp.where(same_h, jnp.concatenate([xc] * (2 * ck // hb), axis=0), 0.0)

    lane_lo = lax.broadcasted_iota(jnp.int32, (ck, RWKV_PAIR), 1) < RWKV_HEAD

    npair = w_ // RWKV_PAIR
    nch = RWKV_CHUNKS_PER_ITER
    sl = [slice(p * RWKV_PAIR, (p + 1) * RWKV_PAIR) for p in range(npair)]

    n2 = 2 * ck
    st = [st_s[p] for p in range(npair)]

    def pre(g, res):
        a_m, b_m, k_m, r_m, v_m, nb_end, k_end, dec, rows = [], [], [], [], [], [], [], [], []
        for j in range(nch):
            rws = slice((g * nch + j) * ck, (g * nch + j + 1) * ck)
            rows.append(rws)
            lw_c = lw_s[rws, :]
            lw_hi = lw_c.astype(BF16)
            lw_lo = (lw_c - lw_hi.astype(F32)).astype(BF16)
            cum = (jnp.dot(tril, lw_hi, preferred_element_type=F32)
                   + jnp.dot(tril, lw_lo, preferred_element_type=F32))
            e_neg = jnp.exp(-cum)
            total = cum[ck - 1:ck]
            e_end = jnp.exp(total - cum)
            e_tot = jnp.exp(total)
            a_all = kk_s[rws, :] * jnp.exp(cum - lw_c)
            b_all = b_s[rws, :]
            k_all = k_s[rws, :]
            r_all = r_s[rws, :] * jnp.exp(cum)
            v_all = v_s[rws, :]
            stk = lambda x: [_stack_pair(x[:, ls], lane_lo).astype(BF16) for ls in sl]
            a_m += stk(a_all)
            b_m += stk(b_all * e_neg)
            k_m += stk(k_all * e_neg)
            r_m += stk(r_all)
            v_m += stk(v_all)
            nb_end += stk(b_all * -e_end)
            k_end += stk(k_all * e_end)
            dec += [jnp.broadcast_to(e_tot[:, ls], (2 * ck, RWKV_PAIR)).T for ls in sl]
        un = range(nch * npair)
        yield
        mm = [_dot_nt(jnp.concatenate([a_m[i], r_m[i]], axis=0),
                      jnp.concatenate([b_m[i], k_m[i]], axis=0)) for i in un]
        yield
        l_ak = [jnp.where(strict, mm[i][:n2, n2:], 0.0) for i in un]
        m_rb = [jnp.where(incl, mm[i][n2:, :n2], 0.0).astype(BF16) for i in un]
        m_rk = [jnp.where(incl, mm[i][n2:, n2:], 0.0).astype(BF16) for i in un]
        dg = [jnp.where(strict_h, mm[i][:n2, :n2], 0.0) for i in un]
        off = [jnp.where(strict_o, mm[i][:n2, :n2], 0.0).astype(BF16) for i in un]
        pc = [compress(dg[i]) for i in un]
        pf = [dg[i].astype(BF16) for i in un]
        tc = [eye_c - pc[i] for i in un]
        yield
        pc = [_dot(pc[i], pf[i]) for i in un]
        yield
        for _i in range(3):
            pf = [expand(pc[i]).astype(BF16) for i in un]
            both = [_dot(jnp.concatenate([pc[i], tc[i]], axis=0), pf[i]) for i in un]
            pc = [both[i][:hb] for i in un]
            tc = [tc[i] + both[i][hb:] for i in un]
            yield
        pf = [expand(pc[i]).astype(BF16) for i in un]
        tc = [tc[i] + _dot(tc[i], pf[i]) for i in un]
        yield
        t_h = [expand(tc[i]) for i in un]
        t_hb = [t_h[i].astype(BF16) for i in un]
        x = [_dot(off[i], t_hb[i]) for i in un]
        yield
        t_inv = [(t_h[i] - _dot(t_hb[i], x[i])).astype(BF16) for i in un]
        yield
        lv = [_dot(l_ak[i], v_m[i]).astype(BF16) for i in un]
        yield
        w12 = [_dot(t_inv[i], jnp.concatenate([a_m[i], lv[i]], axis=1)) for i in un]
        yield
        res.update(
            rows=rows, v_m=v_m, m_rb=m_rb, dec=dec,
            w1=[w12[i][:, :n2].astype(BF16) for i in un], w2=[w12[i][:, n2:] for i in un],
            rk=[jnp.concatenate([r_m[i], m_rk[i]], axis=1) for i in un],
            kbt=[jnp.concatenate([k_end[i].astype(F32).T.astype(BF16),
                                  nb_end[i].astype(F32).T.astype(BF16)], axis=1) for i in un])
        yield

    def seq(res):
        for j in range(nch):
            ids = [j * npair + p for p in range(npair)]
            st_b = [st[p].astype(BF16) for p in range(npair)]
            u = [_dot(res["w1"][i], st_b[p]) + res["w2"][i] for p, i in enumerate(ids)]
            ub = [u[p].astype(BF16) for p in range(npair)]
            yield
            o = [_dot(res["rk"][i], jnp.concatenate([st_b[p], res["v_m"][i]], axis=0))
                 - _dot(res["m_rb"][i], ub[p]) for p, i in enumerate(ids)]
            for p, i in enumerate(ids):
                st[p] = res["dec"][i] * st[p] + _dot(
                    res["kbt"][i], jnp.concatenate([res["v_m"][i], ub[p]], axis=0))
                o_s[res["rows"][j], sl[p]] = o[p][:ck] + o[p][ck:]
            yield

    def drive(main, side, every):
        for n, _ in enumerate(main):
            if side is not None and n % every == every - 1:
                next(side, None)
        for _ in side or ():
            pass

    ngrp = ts // (ck * nch)
    res = [dict() for _ in range(ngrp)]
    drive(pre(0, res[0]), None, 1)
    for grp in range(1, ngrp):
        drive(pre(grp, res[grp]), seq(res[grp - 1]), RWKV_STAGES_PER_UPDATE)
    drive(seq(res[ngrp - 1]), None, 1)
    for p in range(npair):
        st_s[p] = st[p]

    o = o_s[...]
    inv_n = 1.0 / RWKV_HEAD
    mean = _head_sums(o, seg) * inv_n
    d = o - mean
    var = _head_sums(d * d, seg) * inv_n
    gn = d * lax.rsqrt(var + GN_EPS) * lnx_w + lnx_b
    bonus = _head_sums(r * k * r_k, seg) * v
    y_ref[...] = ((gn + bonus) * g).astype(y_ref.dtype)


def rwkv7(p, mu, w_lora, a_lora, g_lora, v_lora, vecs, v_first):
    b, s, cols = p.shape
    vres = v_first is not None
    ts = _pick(s, (512, 256, 128, 64))
    head = jnp.arange(MXU_WIDTH) // RWKV_HEAD
    seg = (head[:, None] == head[None, :]).astype(BF16)
    tile = lambda c: pl.BlockSpec((None, ts, c), lambda i, t: (i, t, 0))
    full = lambda a: pl.BlockSpec(a.shape, lambda i, t: (0,) * a.ndim)
    ins = [p, mu, w_lora, a_lora, g_lora] + ([v_lora] if vres else []) + [vecs, seg]
    in_specs = [tile(cols)] + [full(a) for a in ins[1:]]
    y_shape = jax.ShapeDtypeStruct((b, s, MIX_W), BF16)
    if vres:
        ins.append(v_first)
        in_specs.append(tile(MIX_W))
        out_shape, out_specs = y_shape, tile(MIX_W)
    else:
        out_shape = (y_shape, jax.ShapeDtypeStruct((b, s, MIX_W), F32))
        out_specs = (tile(MIX_W), tile(MIX_W))
    pairs = MIX_W // RWKV_PAIR
    scratch = ([pltpu.VMEM((8, cols), F32)] + [pltpu.VMEM((ts, MIX_W), F32)] * 7
               + [pltpu.VMEM((pairs, RWKV_PAIR, RWKV_PAIR), F32)])
    out = pl.pallas_call(
        functools.partial(_rwkv_kernel, vres=vres),
        grid=(b, s // ts),
        in_specs=in_specs, out_specs=out_specs, out_shape=out_shape,
        scratch_shapes=scratch,
        compiler_params=pltpu.CompilerParams(
            dimension_semantics=("parallel", "arbitrary"), vmem_limit_bytes=VMEM_LIMIT),
        name="rwkv7",
    )(*ins)
    return (out, v_first) if vres else out


def _mix_ca_kernel(cb_ref, cc_ref, cx_ref, cw_ref, ys_ref, yr_ref, gate_ref, gb_ref, bp_ref, wmo_ref,
                   h_ref, g_ref, wq_ref, k_ref, v_ref, wo_ref, o_ref, carry_ref):
    @pl.when(pl.program_id(1) == 0)
    def _():
        carry_ref[...] = jnp.zeros_like(carry_ref)

    u = cc_ref[...].astype(F32) * cx_ref[...].astype(F32)
    prev8 = carry_ref[...]
    conv = (cw_ref[2:3] * u + cw_ref[1:2] * _shift_rows(u, prev8, 1)
            + cw_ref[0:1] * _shift_rows(u, prev8, 2))
    carry_ref[...] = u[u.shape[0] - 8:]
    y_conv = (cb_ref[...].astype(F32) * conv).astype(BF16)

    merged = None
    for n, y in enumerate((y_conv, ys_ref[...], yr_ref[...])):
        branch = jnp.dot(y, bp_ref[n], preferred_element_type=F32)
        gate = jax.nn.sigmoid(gate_ref[:, n * D_MODEL:(n + 1) * D_MODEL].astype(F32) + gb_ref[n:n + 1])
        merged = gate * branch if merged is None else merged + gate * branch
    h = h_ref[...] + _dot(merged, wmo_ref[...])

    q = _dot(_rms(h, g_ref[...]), wq_ref[...]).astype(BF16)
    heads = [slice(hd * CA_HEAD_DIM, (hd + 1) * CA_HEAD_DIM) for hd in range(CA_HEADS)]
    sc = [_dot_nt(q[:, sl], k_ref[:, sl]) * (CA_HEAD_DIM ** -0.5) for sl in heads]
    e = [jnp.exp(x - jnp.max(x, axis=-1, keepdims=True)) for x in sc]
    probs = [x / jnp.sum(x, axis=-1, keepdims=True) for x in e]
    outs = [_dot(pr, v_ref[:, sl]) for pr, sl in zip(probs, heads)]
    o_ref[...] = h + _dot(jnp.concatenate(outs, axis=1), wo_ref[...])


def mix_cross_attention(h, proj_main, y_sb, y_rwkv, conv_w, gate_b, branch_proj, w_mix_out,
                        kv, gain_ca, wq, wo):
    b, s, d = h.shape
    mem = kv.shape[1]
    tm = _pick(s, (512, 256, 128))
    full = lambda a: pl.BlockSpec(a.shape, lambda i, t: (0,) * a.ndim, pipeline_mode=pl.Buffered(1))
    tile = lambda c, j=0: pl.BlockSpec((None, tm, c), lambda i, t, j=j: (i, t, j))
    return pl.pallas_call(
        _mix_ca_kernel,
        grid=(b, s // tm),
        in_specs=[tile(MIX_W, 0), tile(MIX_W, 1), tile(MIX_W, 2), full(conv_w),
                  tile(MIX_W), tile(MIX_W), tile(N_BRANCH * D_MODEL, 1),
                  full(gate_b), full(branch_proj), full(w_mix_out),
                  tile(d), full(gain_ca), full(wq),
                  pl.BlockSpec((None, mem, d), lambda i, t: (i, 0, 0)),
                  pl.BlockSpec((None, mem, d), lambda i, t: (i, 0, 1)),
                  full(wo)],
        out_specs=tile(d),
        out_shape=jax.ShapeDtypeStruct((b, s, d), F32),
        scratch_shapes=[pltpu.VMEM((8, MIX_W), F32)],
        compiler_params=pltpu.CompilerParams(
            dimension_semantics=("parallel", "arbitrary"), vmem_limit_bytes=VMEM_LIMIT),
        name="mix_cross_attention",
    )(proj_main, proj_main, proj_main, conv_w, y_sb, y_rwkv, proj_main, gate_b, branch_proj,
      w_mix_out, h, gain_ca, wq, kv, kv, wo)


def _conv_ffn_kernel(h_ref, g_ref, wup_ref, cw_ref, cb_ref, wdn_ref, gf_ref, o_ref, carry_ref,
                     act_ref, *, fc, final_norm):
    tm = h_ref.shape[0]
    nc = FFN_DIM // fc

    @pl.when(pl.program_id(1) == 0)
    def _():
        carry_ref[...] = jnp.zeros_like(carry_ref)

    h = h_ref[...]
    hn = _rms(h, g_ref[...]).astype(BF16)
    for c in range(nc):
        halves = []
        for part in range(2):
            lo = part * FFN_DIM + c * fc
            u = jnp.dot(hn, wup_ref[:, lo:lo + fc], preferred_element_type=F32)
            prev8 = carry_ref[part * nc + c]
            cw = cw_ref[:, lo:lo + fc]
            halves.append(cw[2:3] * u + cw[1:2] * _shift_rows(u, prev8, 1)
                          + cw[0:1] * _shift_rows(u, prev8, 2) + cb_ref[:, lo:lo + fc])
            carry_ref[part * nc + c] = u[tm - 8:]
        act_ref[:, c * fc:(c + 1) * fc] = (jax.nn.silu(halves[0]) * halves[1]).astype(BF16)
    out = h + jnp.dot(act_ref[...], wdn_ref[...], preferred_element_type=F32)
    o_ref[...] = _rms(out, gf_ref[...]) if final_norm else out


def conv_ffn(h, gain, w_up, conv_w, conv_b, w_down, gain_final, final_norm):
    b, s, d = h.shape
    tm = _pick(s, (512, 256, 128))
    fc = 256
    full = lambda a: pl.BlockSpec(a.shape, lambda i, t: (0,) * a.ndim, pipeline_mode=pl.Buffered(1))
    return pl.pallas_call(
        functools.partial(_conv_ffn_kernel, fc=fc, final_norm=final_norm),
        grid=(b, s // tm),
        in_specs=[pl.BlockSpec((None, tm, d), lambda i, t: (i, t, 0)),
                  full(gain), full(w_up), full(conv_w), full(conv_b), full(w_down), full(gain_final)],
        out_specs=pl.BlockSpec((None, tm, d), lambda i, t: (i, t, 0)),
        out_shape=jax.ShapeDtypeStruct((b, s, d), F32),
        scratch_shapes=[pltpu.VMEM((2 * FFN_DIM // fc, 8, fc), F32), pltpu.VMEM((tm, FFN_DIM), BF16)],
        compiler_params=pltpu.CompilerParams(
            dimension_semantics=("parallel", "arbitrary"), vmem_limit_bytes=FFN_VMEM_LIMIT),
        name="conv_ffn",
    )(h, gain, w_up, conv_w, conv_b, w_down, gain_final)


def _pad_rows(w, rows):
    return jnp.concatenate([w, jnp.zeros((rows - w.shape[0],) + w.shape[1:], w.dtype)], axis=0)


def kernel(x, mem, norm_mix, w_comb, conv_w, mu_rwkv, w0, w_lora, a0, a_lora, g_lora, k_k, k_a, r_k, lnx_w, lnx_b, w_vres, mu_vres, v0, v_lora, branch_proj, gate_b, w_mix_out, norm_ca, norm_mem, ca_wq, ca_wkv, ca_wo, norm_ffn, ffn_up, ffn_conv_w, ffn_conv_b, ffn_down, norm_final):
    b, s, d = x.shape
    depth = w_comb.shape[0]
    mem_len = mem.shape[1]
    h = x
    v_first = None
    zero_lora = jnp.zeros((DECAY_LORA, MIX_W), F32)
    for l in range(depth):
        h2 = h.reshape(b * s, d)
        w_main = w_comb[l, :, :MAIN_COLS].astype(BF16)
        w_rwkv = w_comb[l, :, MAIN_COLS:]
        mu = mu_rwkv[l]
        if l > 0:
            pad = LANES - VRES_LORA
            w_rwkv = jnp.concatenate([w_rwkv, w_vres[l - 1], jnp.zeros((d, pad), F32)], axis=1)
            mu = jnp.concatenate([mu, mu_vres[l - 1], jnp.zeros((pad,), F32)])
        proj_main, proj_rwkv = input_projection(h2, norm_mix[l], w_main, w_rwkv.astype(BF16))
        proj_main = proj_main.reshape(b, s, MAIN_COLS)
        proj_rwkv = proj_rwkv.reshape(b, s, -1)

        y_sb = stick_breaking(proj_main)
        vecs = jnp.stack([w0[l], a0[l], k_k[l], k_a[l], r_k[l].reshape(-1), lnx_w[l], lnx_b[l],
                          v0[l - 1] if l > 0 else jnp.zeros((MIX_W,), F32)])
        y_rwkv, v_first = rwkv7(
            proj_rwkv, mu.reshape(1, -1),
            jnp.concatenate([w_lora[l], zero_lora], axis=0).astype(BF16),
            jnp.concatenate([zero_lora, a_lora[l]], axis=0).astype(BF16),
            g_lora[l].astype(BF16),
            _pad_rows(v_lora[l - 1], LANES).astype(BF16) if l > 0 else None,
            vecs, v_first)

        kv = norm_matmul(mem.reshape(b * mem_len, d), norm_mem[l], ca_wkv[l].astype(BF16), BF16)
        h = mix_cross_attention(h, proj_main, y_sb, y_rwkv, conv_w[l], gate_b[l],
                                branch_proj[l].astype(BF16), w_mix_out[l].astype(BF16),
                                kv.reshape(b, mem_len, 2 * d), norm_ca[l].reshape(1, d),
                                ca_wq[l].astype(BF16), ca_wo[l].astype(BF16))

        h = conv_ffn(h, norm_ffn[l].reshape(1, d), ffn_up[l].astype(BF16), ffn_conv_w[l],
                     ffn_conv_b[l].reshape(1, -1), ffn_down[l].astype(BF16),
                     norm_final.reshape(1, d), final_norm=(l == depth - 1))
    return h
```

```python
import functools
import math

import jax
import jax.numpy as jnp
from jax import lax
from jax.experimental import pallas as pl
from jax.experimental.pallas import tpu as pltpu

F32 = jnp.float32
BF16 = jnp.bfloat16

D_MODEL = 1024
MIX_W = 512
N_BRANCH = 3
SB_HEADS = 8
SB_HEAD_DIM = MIX_W // SB_HEADS
RWKV_HEADS = 8
RWKV_HEAD = MIX_W // RWKV_HEADS
DECAY_LORA = 64
VRES_LORA = 32
DECAY_SCALE = math.exp(-0.5)
CA_HEADS = 4
CA_HEAD_DIM = D_MODEL // CA_HEADS
FFN_DIM = 2816
RMS_EPS = 1e-6
GN_EPS = 64e-5

MAIN_COLS = 3 * MIX_W + 3 * MIX_W + N_BRANCH * D_MODEL
LANES = 128
SUBLANES = 8
VMEM_LIMIT = 48 * 1024 * 1024
FFN_VMEM_LIMIT = 56 * 1024 * 1024

SB_BLOCK = 128
SB_QBLOCKS = 4
LOG2E = math.log2(math.e)
SB_DEAD_LOG2 = 104.0 * LOG2E
RWKV_CHUNK = 64
RWKV_PAIR = 2 * RWKV_HEAD
RWKV_CHUNKS_PER_ITER = 4
RWKV_STAGES_PER_UPDATE = 1
MXU_WIDTH = 256


def _dot(a, b):
    return jnp.dot(a.astype(BF16), b.astype(BF16), preferred_element_type=F32)


def _dot_nt(a, b):
    return lax.dot_general(a.astype(BF16), b.astype(BF16), (((1,), (1,)), ((), ())),
                           preferred_element_type=F32)


def _head_sums(x, seg):
    xb = x.astype(BF16)
    n = seg.shape[0]
    return jnp.concatenate([jnp.dot(xb[:, i:i + n], seg, preferred_element_type=F32)
                            for i in range(0, x.shape[1], n)], axis=1)


def _rms(x, g):
    ms = jnp.mean(x * x, axis=-1, keepdims=True)
    return x * lax.rsqrt(ms + RMS_EPS) * g


def _shift_rows(u, prev8, n):
    r = pltpu.roll(u, n, 0)
    rows = lax.broadcasted_iota(jnp.int32, prev8.shape, 0)
    head = jnp.where(rows < n, pltpu.roll(prev8, n, 0), r[:SUBLANES])
    return jnp.concatenate([head, r[SUBLANES:]], axis=0)


def _pick(n, cands):
    for c in cands:
        if n % c == 0:
            return c
    raise ValueError(f"no tile for {n} in {cands}")


def _norm_mm_kernel(x_ref, g_ref, w_ref, o_ref, xn_ref):
    @pl.when(pl.program_id(1) == 0)
    def _():
        xn_ref[...] = _rms(x_ref[...], g_ref[...]).astype(BF16)

    o_ref[...] = jnp.dot(xn_ref[...], w_ref[...], preferred_element_type=F32).astype(o_ref.dtype)


def norm_matmul(x, gain, w, out_dtype):
    m, k = x.shape
    n = w.shape[1]
    tm = _pick(m, (1024, 512, 256, 128))
    tn = _pick(n, (512, 384, 256, 128))
    return pl.pallas_call(
        _norm_mm_kernel,
        grid=(m // tm, n // tn),
        in_specs=[pl.BlockSpec((tm, k), lambda i, j: (i, 0)),
                  pl.BlockSpec((1, k), lambda i, j: (0, 0)),
                  pl.BlockSpec((k, tn), lambda i, j: (0, j))],
        out_specs=pl.BlockSpec((tm, tn), lambda i, j: (i, j)),
        out_shape=jax.ShapeDtypeStruct((m, n), out_dtype),
        scratch_shapes=[pltpu.VMEM((tm, k), BF16)],
        compiler_params=pltpu.CompilerParams(
            dimension_semantics=("parallel", "arbitrary"), vmem_limit_bytes=VMEM_LIMIT),
        name="norm_matmul",
    )(x, gain.reshape(1, k), w)


def _in_proj_kernel(x_ref, g_ref, wm_ref, wr_ref, om_ref, or_ref):
    xn = _rms(x_ref[...], g_ref[...]).astype(BF16)
    for w_ref, o_ref in ((wm_ref, om_ref), (wr_ref, or_ref)):
        cols = w_ref.shape[1]
        for lo in range(0, cols, 2 * MXU_WIDTH):
            n = min(2 * MXU_WIDTH, cols - lo)
            o_ref[:, lo:lo + n] = jnp.dot(xn, w_ref[:, lo:lo + n],
                                          preferred_element_type=F32).astype(o_ref.dtype)


def input_projection(x, gain, w_main, w_rwkv):
    m, k = x.shape
    tm = _pick(m, (512, 256, 128))
    nm, nr = w_main.shape[1], w_rwkv.shape[1]
    gain = gain.reshape(1, k)
    full = lambda a: pl.BlockSpec(a.shape, lambda i: (0,) * a.ndim, pipeline_mode=pl.Buffered(1))
    row = lambda c: pl.BlockSpec((tm, c), lambda i: (i, 0))
    return pl.pallas_call(
        _in_proj_kernel,
        grid=(m // tm,),
        in_specs=[row(k), full(gain), full(w_main), full(w_rwkv)],
        out_specs=(row(nm), row(nr)),
        out_shape=(jax.ShapeDtypeStruct((m, nm), BF16), jax.ShapeDtypeStruct((m, nr), F32)),
        compiler_params=pltpu.CompilerParams(
            dimension_semantics=("parallel",), vmem_limit_bytes=FFN_VMEM_LIMIT),
        name="input_projection",
    )(x, gain, w_main, w_rwkv)


def _stack_pair(x, lane_lo):
    z = jnp.zeros_like(x)
    return jnp.concatenate([jnp.where(lane_lo, x, z), jnp.where(lane_lo, z, x)], axis=0)


def _sb_kernel(q_ref, k_ref, v_ref, tri_ref, o_ref, acc_ref, c_ref, ks_ref, vs_ref):
    tq = SB_BLOCK
    nq = SB_QBLOCKS
    npair = MIX_W // LANES
    zero_blk = ks_ref.shape[0] - npair
    units = [(a, p) for a in range(nq) for p in range(npair)]
    un = range(len(units))
    step = pl.program_id(1)
    lane_lo = lax.broadcasted_iota(jnp.int32, (tq, LANES), 1) < SB_HEAD_DIM
    row = lax.broadcasted_iota(jnp.int32, (tq, 2 * tq), 0)
    col = lax.broadcasted_iota(jnp.int32, (tq, 2 * tq), 1)
    causal = jnp.where(col < tq, col, col - tq) < row
    halves = [slice(0, tq), slice(tq, 2 * tq)]
    tri = tri_ref[...]
    sign = jnp.uint32(0x80000000)

    @pl.when(step == 0)
    def _():
        for p in range(npair):
            ks_ref[zero_blk + p] = jnp.zeros(ks_ref.shape[1:], BF16)
            vs_ref[zero_blk + p] = jnp.zeros(vs_ref.shape[1:], BF16)

    q = []
    for a, p in units:
        rs, ls = slice(a * tq, (a + 1) * tq), slice(p * LANES, (p + 1) * LANES)
        q.append(q_ref[rs, ls] * (SB_HEAD_DIM ** -0.5))
        ks_ref[(step * nq + a) * npair + p] = _stack_pair(k_ref[rs, ls], lane_lo)
        vs_ref[(step * nq + a) * npair + p] = _stack_pair(v_ref[rs, ls], lane_lo)

    def walk(i, diag):
        jb = [step * nq + a - i for a in range(nq)]
        blk = [jnp.where(jb[a] >= 0, jb[a] * npair, zero_blk) + p for a, p in units]
        ks = [ks_ref[blk[u]] for u in un]
        vs = [vs_ref[blk[u]] for u in un]
        z = [_dot_nt(q[u], ks[u]) * LOG2E for u in un]
        nabs = [pltpu.bitcast(pltpu.bitcast(z[u], jnp.uint32) | sign, F32) for u in un]
        sp = [jnp.maximum(z[u], 0.0) + jnp.log(1.0 + jnp.exp2(nabs[u])) * LOG2E for u in un]
        lsig = [z[u] - sp[u] for u in un]
        if diag:
            sp = [jnp.where(causal, sp[u], 0.0) for u in un]
        hi = [sp[u].astype(BF16) for u in un]
        lo = [(sp[u] - hi[u].astype(F32)).astype(BF16) for u in un]
        suf = [jnp.dot(jnp.concatenate([hi[u], lo[u]], axis=1), tri, preferred_element_type=F32)
               for u in un]
        c = [[c_ref[u, e] for e in range(2)] for u in un]
        att = [jnp.concatenate([jnp.exp2(lsig[u][:, hs] - suf[u][:, hs] - c[u][e])
                                for e, hs in enumerate(halves)], axis=1) for u in un]
        if diag:
            att = [jnp.where(causal, att[u], 0.0) for u in un]
        cmin = [None] * nq
        for u, (a, p) in enumerate(units):
            acc_ref[u] += jnp.dot(att[u].astype(BF16), vs[u], preferred_element_type=F32)
            for e, hs in enumerate(halves):
                cn = c[u][e] + jnp.sum(sp[u][:, hs], axis=1, keepdims=True)
                c_ref[u, e] = cn
                cmin[a] = cn if cmin[a] is None else jnp.minimum(cmin[a], cn)
        alive = [jnp.logical_and(jb[a] >= 1, jnp.min(cmin[a]) <= SB_DEAD_LOG2) for a in range(nq)]
        return functools.reduce(jnp.logical_or, alive)

    acc_ref[...] = jnp.zeros_like(acc_ref)
    c_ref[...] = jnp.zeros_like(c_ref)
    alive0 = walk(0, True)
    lax.while_loop(lambda s: s[1], lambda s: (s[0] + 1, walk(s[0], False)), (1, alive0))
    for a in range(nq):
        o_ref[a * tq:(a + 1) * tq, :] = jnp.concatenate(
            [acc_ref[a * npair + p] for p in range(npair)], axis=1).astype(o_ref.dtype)


def stick_breaking(proj_main):
    b, s, _ = proj_main.shape
    tq = SB_BLOCK
    pairs = MIX_W // LANES
    half = jnp.arange(2 * tq) // tq
    pos = jnp.arange(2 * tq)
    same = half[:, None] == half[None, :]
    later = same & (pos[:, None] > pos[None, :])
    tri = jnp.concatenate([later, later], axis=0).astype(BF16)
    nq = SB_QBLOCKS
    blk = lambda c: pl.BlockSpec((None, nq * tq, MIX_W), lambda i, t, c=c: (i, t, c))
    kv_blocks = (s // tq + 1) * pairs
    return pl.pallas_call(
        _sb_kernel,
        grid=(b, s // (nq * tq)),
        in_specs=[blk(3), blk(4), blk(5), pl.BlockSpec(tri.shape, lambda i, t: (0, 0))],
        out_specs=blk(0),
        out_shape=jax.ShapeDtypeStruct((b, s, MIX_W), BF16),
        scratch_shapes=[pltpu.VMEM((nq * pairs, tq, LANES), F32),
                        pltpu.VMEM((nq * pairs, 2, tq, tq), F32),
                        pltpu.VMEM((kv_blocks, 2 * tq, LANES), BF16),
                        pltpu.VMEM((kv_blocks, 2 * tq, LANES), BF16)],
        compiler_params=pltpu.CompilerParams(
            dimension_semantics=("parallel", "arbitrary"), vmem_limit_bytes=VMEM_LIMIT),
        name="stick_breaking",
    )(proj_main, proj_main, proj_main, tri)


def _rwkv_kernel(*refs, vres):
    if vres:
        (p_ref, mu_ref, wl_ref, al_ref, gl_ref, vl_ref, vec_ref, seg_ref, vfirst_ref,
         y_ref, carry_ref, r_s, lw_s, k_s, v_s, kk_s, b_s, o_s, st_s) = refs
    else:
        (p_ref, mu_ref, wl_ref, al_ref, gl_ref, vec_ref, seg_ref,
         y_ref, vfirst_ref, carry_ref, r_s, lw_s, k_s, v_s, kk_s, b_s, o_s, st_s) = refs
    ts = p_ref.shape[0]
    ck = RWKV_CHUNK
    w_ = MIX_W

    @pl.when(pl.program_id(1) == 0)
    def _():
        carry_ref[...] = jnp.zeros_like(carry_ref)
        st_s[...] = jnp.zeros_like(st_s)

    p = p_ref[...]
    prev = _shift_rows(p, carry_ref[...], 1)
    carry_ref[...] = p[ts - SUBLANES:]
    p = p + mu_ref[...] * (prev - p)
    w0, a0, k_k, k_a, r_k, lnx_w, lnx_b, v0 = (vec_ref[i:i + 1] for i in range(8))
    seg = seg_ref[...]
    r = p[:, :w_]
    k = p[:, w_:2 * w_]
    v = p[:, 2 * w_:3 * w_]
    x_wa = p[:, 3 * w_:3 * w_ + LANES]
    x_g = p[:, 3 * w_ + LANES:3 * w_ + 2 * LANES]
    lw = -DECAY_SCALE * jax.nn.sigmoid(w0 + _dot(jnp.tanh(x_wa), wl_ref[...]))
    a = jax.nn.sigmoid(a0 + _dot(x_wa, al_ref[...]))
    if vres:
        x_v = p[:, 3 * w_ + 2 * LANES:3 * w_ + 3 * LANES]
        v = v + (vfirst_ref[...] - v) * jax.nn.sigmoid(v0 + _dot(x_v, vl_ref[...]))
    else:
        vfirst_ref[...] = v
    g = _dot(jax.nn.sigmoid(x_g), gl_ref[...])
    kk = k * k_k
    kk = kk / jnp.maximum(jnp.sqrt(_head_sums(kk * kk, seg)), 1e-12)
    k = k * (1.0 + (a - 1.0) * k_a)
    r_s[...] = r
    lw_s[...] = lw
    k_s[...] = k
    v_s[...] = v
    kk_s[...] = kk
    b_s[...] = kk * a

    rc = lax.broadcasted_iota(jnp.int32, (ck, ck), 0)
    cc = lax.broadcasted_iota(jnp.int32, (ck, ck), 1)
    tril = jnp.where(cc <= rc, 1.0, 0.0).astype(BF16)
    r2 = lax.broadcasted_iota(jnp.int32, (2 * ck, 2 * ck), 0)
    c2 = lax.broadcasted_iota(jnp.int32, (2 * ck, 2 * ck), 1)
    same = (r2 // ck) == (c2 // ck)
    strict = jnp.logical_and(same, c2 < r2)
    incl = jnp.logical_and(same, c2 <= r2)
    hb = ck // 2
    same_h = (r2 // hb) == (c2 // hb)
    strict_h = jnp.logical_and(same_h, c2 < r2)
    strict_o = jnp.logical_and(strict, jnp.logical_not(same_h))
    rh = lax.broadcasted_iota(jnp.int32, (hb, 2 * ck), 0)
    ch = lax.broadcasted_iota(jnp.int32, (hb, 2 * ck), 1)
    eye_c = jnp.where(ch % hb == rh, 1.0, 0.0)

    def compress(x):
        return functools.reduce(lambda a, b: a + b, [x[i:i + hb] for i in range(0, 2 * ck, hb)])

    def expand(xc):
        return jnp.where(same_h, jnp.concatenate([xc] * (2 * ck // hb), axis=0), 0.0)

    lane_lo = lax.broadcasted_iota(jnp.int32, (ck, RWKV_PAIR), 1) < RWKV_HEAD

    npair = w_ // RWKV_PAIR
    nch = RWKV_CHUNKS_PER_ITER
    sl = [slice(p * RWKV_PAIR, (p + 1) * RWKV_PAIR) for p in range(npair)]

    n2 = 2 * ck
    st = [st_s[p] for p in range(npair)]

    def pre(g, res):
        a_m, b_m, k_m, r_m, v_m, nb_end, k_end, dec, rows = [], [], [], [], [], [], [], [], []
        for j in range(nch):
            rws = slice((g * nch + j) * ck, (g * nch + j + 1) * ck)
            rows.append(rws)
            lw_c = lw_s[rws, :]
            lw_hi = lw_c.astype(BF16)
            lw_lo = (lw_c - lw_hi.astype(F32)).astype(BF16)
            cum = (jnp.dot(tril, lw_hi, preferred_element_type=F32)
                   + jnp.dot(tril, lw_lo, preferred_element_type=F32))
            e_neg = jnp.exp(-cum)
            total = cum[ck - 1:ck]
            e_end = jnp.exp(total - cum)
            e_tot = jnp.exp(total)
            a_all = kk_s[rws, :] * jnp.exp(cum - lw_c)
            b_all = b_s[rws, :]
            k_all = k_s[rws, :]
            r_all = r_s[rws, :] * jnp.exp(cum)
            v_all = v_s[rws, :]
            stk = lambda x: [_stack_pair(x[:, ls], lane_lo).astype(BF16) for ls in sl]
            a_m += stk(a_all)
            b_m += stk(b_all * e_neg)
            k_m += stk(k_all * e_neg)
            r_m += stk(r_all)
            v_m += stk(v_all)
            nb_end += stk(b_all * -e_end)
            k_end += stk(k_all * e_end)
            dec += [jnp.broadcast_to(e_tot[:, ls], (2 * ck, RWKV_PAIR)).T for ls in sl]
        un = range(nch * npair)
        yield
        mm = [_dot_nt(jnp.concatenate([a_m[i], r_m[i]], axis=0),
                      jnp.concatenate([b_m[i], k_m[i]], axis=0)) for i in un]
        yield
        l_ak = [jnp.where(strict, mm[i][:n2, n2:], 0.0) for i in un]
        m_rb = [jnp.where(incl, mm[i][n2:, :n2], 0.0).astype(BF16) for i in un]
        m_rk = [jnp.where(incl, mm[i][n2:, n2:], 0.0).astype(BF16) for i in un]
        dg = [jnp.where(strict_h, mm[i][:n2, :n2], 0.0) for i in un]
        off = [jnp.where(strict_o, mm[i][:n2, :n2], 0.0).astype(BF16) for i in un]
        pc = [compress(dg[i]) for i in un]
        pf = [dg[i].astype(BF16) for i in un]
        tc = [eye_c - pc[i] for i in un]
        yield
        pc = [_dot(pc[i], pf[i]) for i in un]
        yield
        for _i in range(3):
            pf = [expand(pc[i]).astype(BF16) for i in un]
            both = [_dot(jnp.concatenate([pc[i], tc[i]], axis=0), pf[i]) for i in un]
            pc = [both[i][:hb] for i in un]
            tc = [tc[i] + both[i][hb:] for i in un]
            yield
        pf = [expand(pc[i]).astype(BF16) for i in un]
        tc = [tc[i] + _dot(tc[i], pf[i]) for i in un]
        yield
        t_h = [expand(tc[i]) for i in un]
        t_hb = [t_h[i].astype(BF16) for i in un]
        x = [_dot(off[i], t_hb[i]) for i in un]
        yield
        t_inv = [(t_h[i] - _dot(t_hb[i], x[i])).astype(BF16) for i in un]
        yield
        lv = [_dot(l_ak[i], v_m[i]).astype(BF16) for i in un]
        yield
        w12 = [_dot(t_inv[i], jnp.concatenate([a_m[i], lv[i]], axis=1)) for i in un]
        yield
        res.update(
            rows=rows, v_m=v_m, m_rb=m_rb, dec=dec,
            w1=[w12[i][:, :n2].astype(BF16) for i in un], w2=[w12[i][:, n2:] for i in un],
            rk=[jnp.concatenate([r_m[i], m_rk[i]], axis=1) for i in un],
            kbt=[jnp.concatenate([k_end[i].astype(F32).T.astype(BF16),
                                  nb_end[i].astype(F32).T.astype(BF16)], axis=1) for i in un])
        yield

    def seq(res):
        for j in range(nch):
            ids = [j * npair + p for p in range(npair)]
            st_b = [st[p].astype(BF16) for p in range(npair)]
            u = [_dot(res["w1"][i], st_b[p]) + res["w2"][i] for p, i in enumerate(ids)]
            ub = [u[p].astype(BF16) for p in range(npair)]
            yield
            o = [_dot(res["rk"][i], jnp.concatenate([st_b[p], res["v_m"][i]], axis=0))
                 - _dot(res["m_rb"][i], ub[p]) for p, i in enumerate(ids)]
            for p, i in enumerate(ids):
                st[p] = res["dec"][i] * st[p] + _dot(
                    res["kbt"][i], jnp.concatenate([res["v_m"][i], ub[p]], axis=0))
                o_s[res["rows"][j], sl[p]] = o[p][:ck] + o[p][ck:]
            yield

    def drive(main, side, every):
        for n, _ in enumerate(main):
            if side is not None and n % every == every - 1:
                next(side, None)
        for _ in side or ():
            pass

    ngrp = ts // (ck * nch)
    res = [dict() for _ in range(ngrp)]
    drive(pre(0, res[0]), None, 1)
    for grp in range(1, ngrp):
        drive(pre(grp, res[grp]), seq(res[grp - 1]), RWKV_STAGES_PER_UPDATE)
    drive(seq(res[ngrp - 1]), None, 1)
    for p in range(npair):
        st_s[p] = st[p]

    o = o_s[...]
    inv_n = 1.0 / RWKV_HEAD
    mean = _head_sums(o, seg) * inv_n
    d = o - mean
    var = _head_sums(d * d, seg) * inv_n
    gn = d * lax.rsqrt(var + GN_EPS) * lnx_w + lnx_b
    bonus = _head_sums(r * k * r_k, seg) * v
    y_ref[...] = ((gn + bonus) * g).astype(y_ref.dtype)


def rwkv7(p, mu, w_lora, a_lora, g_lora, v_lora, vecs, v_first):
    b, s, cols = p.shape
    vres = v_first is not None
    ts = _pick(s, (512, 256, 128, 64))
    head = jnp.arange(MXU_WIDTH) // RWKV_HEAD
    seg = (head[:, None] == head[None, :]).astype(BF16)
    tile = lambda c: pl.BlockSpec((None, ts, c), lambda i, t: (i, t, 0))
    full = lambda a: pl.BlockSpec(a.shape, lambda i, t: (0,) * a.ndim)
    ins = [p, mu, w_lora, a_lora, g_lora] + ([v_lora] if vres else []) + [vecs, seg]
    in_specs = [tile(cols)] + [full(a) for a in ins[1:]]
    y_shape = jax.ShapeDtypeStruct((b, s, MIX_W), BF16)
    if vres:
        ins.append(v_first)
        in_specs.append(tile(MIX_W))
        out_shape, out_specs = y_shape, tile(MIX_W)
    else:
        out_shape = (y_shape, jax.ShapeDtypeStruct((b, s, MIX_W), F32))
        out_specs = (tile(MIX_W), tile(MIX_W))
    pairs = MIX_W // RWKV_PAIR
    scratch = ([pltpu.VMEM((SUBLANES, cols), F32)] + [pltpu.VMEM((ts, MIX_W), F32)] * 7
               + [pltpu.VMEM((pairs, RWKV_PAIR, RWKV_PAIR), F32)])
    out = pl.pallas_call(
        functools.partial(_rwkv_kernel, vres=vres),
        grid=(b, s // ts),
        in_specs=in_specs, out_specs=out_specs, out_shape=out_shape,
        scratch_shapes=scratch,
        compiler_params=pltpu.CompilerParams(
            dimension_semantics=("parallel", "arbitrary"), vmem_limit_bytes=VMEM_LIMIT),
        name="rwkv7",
    )(*ins)
    return (out, v_first) if vres else out


def _mix_ca_kernel(cb_ref, cc_ref, cx_ref, cw_ref, ys_ref, yr_ref, gate_ref, gb_ref, bp_ref, wmo_ref,
                   h_ref, g_ref, wq_ref, k_ref, v_ref, wo_ref, o_ref, carry_ref):
    @pl.when(pl.program_id(1) == 0)
    def _():
        carry_ref[...] = jnp.zeros_like(carry_ref)

    u = cc_ref[...].astype(F32) * cx_ref[...].astype(F32)
    prev8 = carry_ref[...]
    conv = (cw_ref[2:3] * u + cw_ref[1:2] * _shift_rows(u, prev8, 1)
            + cw_ref[0:1] * _shift_rows(u, prev8, 2))
    carry_ref[...] = u[u.shape[0] - SUBLANES:]
    y_conv = (cb_ref[...].astype(F32) * conv).astype(BF16)

    merged = None
    for n, y in enumerate((y_conv, ys_ref[...], yr_ref[...])):
        branch = jnp.dot(y, bp_ref[n], preferred_element_type=F32)
        gate = jax.nn.sigmoid(gate_ref[:, n * D_MODEL:(n + 1) * D_MODEL].astype(F32) + gb_ref[n:n + 1])
        merged = gate * branch if merged is None else merged + gate * branch
    h = h_ref[...] + _dot(merged, wmo_ref[...])

    q = _dot(_rms(h, g_ref[...]), wq_ref[...]).astype(BF16)
    heads = [slice(hd * CA_HEAD_DIM, (hd + 1) * CA_HEAD_DIM) for hd in range(CA_HEADS)]
    sc = [_dot_nt(q[:, sl], k_ref[:, sl]) * (CA_HEAD_DIM ** -0.5) for sl in heads]
    e = [jnp.exp(x - jnp.max(x, axis=-1, keepdims=True)) for x in sc]
    probs = [x / jnp.sum(x, axis=-1, keepdims=True) for x in e]
    outs = [_dot(pr, v_ref[:, sl]) for pr, sl in zip(probs, heads)]
    o_ref[...] = h + _dot(jnp.concatenate(outs, axis=1), wo_ref[...])


def mix_cross_attention(h, proj_main, y_sb, y_rwkv, conv_w, gate_b, branch_proj, w_mix_out,
                        kv, gain_ca, wq, wo):
    b, s, d = h.shape
    mem = kv.shape[1]
    tm = _pick(s, (512, 256, 128))
    full = lambda a: pl.BlockSpec(a.shape, lambda i, t: (0,) * a.ndim, pipeline_mode=pl.Buffered(1))
    tile = lambda c, j=0: pl.BlockSpec((None, tm, c), lambda i, t, j=j: (i, t, j))
    return pl.pallas_call(
        _mix_ca_kernel,
        grid=(b, s // tm),
        in_specs=[tile(MIX_W, 0), tile(MIX_W, 1), tile(MIX_W, 2), full(conv_w),
                  tile(MIX_W), tile(MIX_W), tile(N_BRANCH * D_MODEL, 1),
                  full(gate_b), full(branch_proj), full(w_mix_out),
                  tile(d), full(gain_ca), full(wq),
                  pl.BlockSpec((None, mem, d), lambda i, t: (i, 0, 0)),
                  pl.BlockSpec((None, mem, d), lambda i, t: (i, 0, 1)),
                  full(wo)],
        out_specs=tile(d),
        out_shape=jax.ShapeDtypeStruct((b, s, d), F32),
        scratch_shapes=[pltpu.VMEM((SUBLANES, MIX_W), F32)],
        compiler_params=pltpu.CompilerParams(
            dimension_semantics=("parallel", "arbitrary"), vmem_limit_bytes=VMEM_LIMIT),
        name="mix_cross_attention",
    )(proj_main, proj_main, proj_main, conv_w, y_sb, y_rwkv, proj_main, gate_b, branch_proj,
      w_mix_out, h, gain_ca, wq, kv, kv, wo)


def _conv_ffn_kernel(h_ref, g_ref, wup_ref, cw_ref, cb_ref, wdn_ref, gf_ref, o_ref, carry_ref,
                     act_ref, *, fc, final_norm):
    tm = h_ref.shape[0]
    nc = FFN_DIM // fc

    @pl.when(pl.program_id(1) == 0)
    def _():
        carry_ref[...] = jnp.zeros_like(carry_ref)

    h = h_ref[...]
    hn = _rms(h, g_ref[...]).astype(BF16)
    for c in range(nc):
        halves = []
        for part in range(2):
            lo = part * FFN_DIM + c * fc
            u = jnp.dot(hn, wup_ref[:, lo:lo + fc], preferred_element_type=F32)
            prev8 = carry_ref[part * nc + c]
            cw = cw_ref[:, lo:lo + fc]
            halves.append(cw[2:3] * u + cw[1:2] * _shift_rows(u, prev8, 1)
                          + cw[0:1] * _shift_rows(u, prev8, 2) + cb_ref[:, lo:lo + fc])
            carry_ref[part * nc + c] = u[tm - SUBLANES:]
        act_ref[:, c * fc:(c + 1) * fc] = (jax.nn.silu(halves[0]) * halves[1]).astype(BF16)
    out = h + jnp.dot(act_ref[...], wdn_ref[...], preferred_element_type=F32)
    o_ref[...] = _rms(out, gf_ref[...]) if final_norm else out


def conv_ffn(h, gain, w_up, conv_w, conv_b, w_down, gain_final, final_norm):
    b, s, d = h.shape
    tm = _pick(s, (512, 256, 128))
    fc = 256
    full = lambda a: pl.BlockSpec(a.shape, lambda i, t: (0,) * a.ndim, pipeline_mode=pl.Buffered(1))
    return pl.pallas_call(
        functools.partial(_conv_ffn_kernel, fc=fc, final_norm=final_norm),
        grid=(b, s // tm),
        in_specs=[pl.BlockSpec((None, tm, d), lambda i, t: (i, t, 0)),
                  full(gain), full(w_up), full(conv_w), full(conv_b), full(w_down), full(gain_final)],
        out_specs=pl.BlockSpec((None, tm, d), lambda i, t: (i, t, 0)),
        out_shape=jax.ShapeDtypeStruct((b, s, d), F32),
        scratch_shapes=[pltpu.VMEM((2 * FFN_DIM // fc, SUBLANES, fc), F32),
                        pltpu.VMEM((tm, FFN_DIM), BF16)],
        compiler_params=pltpu.CompilerParams(
            dimension_semantics=("parallel", "arbitrary"), vmem_limit_bytes=FFN_VMEM_LIMIT),
        name="conv_ffn",
    )(h, gain, w_up, conv_w, conv_b, w_down, gain_final)


def _pad_rows(w, rows):
    return jnp.concatenate([w, jnp.zeros((rows - w.shape[0],) + w.shape[1:], w.dtype)], axis=0)


def kernel(x, mem, norm_mix, w_comb, conv_w, mu_rwkv, w0, w_lora, a0, a_lora, g_lora, k_k, k_a, r_k, lnx_w, lnx_b, w_vres, mu_vres, v0, v_lora, branch_proj, gate_b, w_mix_out, norm_ca, norm_mem, ca_wq, ca_wkv, ca_wo, norm_ffn, ffn_up, ffn_conv_w, ffn_conv_b, ffn_down, norm_final):
    b, s, d = x.shape
    depth = w_comb.shape[0]
    mem_len = mem.shape[1]
    h = x
    v_first = None
    zero_lora = jnp.zeros((DECAY_LORA, MIX_W), F32)
    for l in range(depth):
        h2 = h.reshape(b * s, d)
        w_main = w_comb[l, :, :MAIN_COLS].astype(BF16)
        w_rwkv = w_comb[l, :, MAIN_COLS:]
        mu = mu_rwkv[l]
        if l > 0:
            pad = LANES - VRES_LORA
            w_rwkv = jnp.concatenate([w_rwkv, w_vres[l - 1], jnp.zeros((d, pad), F32)], axis=1)
            mu = jnp.concatenate([mu, mu_vres[l - 1], jnp.zeros((pad,), F32)])
        proj_main, proj_rwkv = input_projection(h2, norm_mix[l], w_main, w_rwkv.astype(BF16))
        proj_main = proj_main.reshape(b, s, MAIN_COLS)
        proj_rwkv = proj_rwkv.reshape(b, s, -1)

        y_sb = stick_breaking(proj_main)
        vecs = jnp.stack([w0[l], a0[l], k_k[l], k_a[l], r_k[l].reshape(-1), lnx_w[l], lnx_b[l],
                          v0[l - 1] if l > 0 else jnp.zeros((MIX_W,), F32)])
        y_rwkv, v_first = rwkv7(
            proj_rwkv, mu.reshape(1, -1),
            jnp.concatenate([w_lora[l], zero_lora], axis=0).astype(BF16),
            jnp.concatenate([zero_lora, a_lora[l]], axis=0).astype(BF16),
            g_lora[l].astype(BF16),
            _pad_rows(v_lora[l - 1], LANES).astype(BF16) if l > 0 else None,
            vecs, v_first)

        kv = norm_matmul(mem.reshape(b * mem_len, d), norm_mem[l], ca_wkv[l].astype(BF16), BF16)
        h = mix_cross_attention(h, proj_main, y_sb, y_rwkv, conv_w[l], gate_b[l],
                                branch_proj[l].astype(BF16), w_mix_out[l].astype(BF16),
                                kv.reshape(b, mem_len, 2 * d), norm_ca[l].reshape(1, d),
                                ca_wq[l].astype(BF16), ca_wo[l].astype(BF16))

        h = conv_ffn(h, norm_ffn[l].reshape(1, d), ffn_up[l].astype(BF16), ffn_conv_w[l],
                     ffn_conv_b[l].reshape(1, -1), ffn_down[l].astype(BF16),
                     norm_final.reshape(1, d), final_norm=(l == depth - 1))
    return h
```

```python
import functools
import math

import jax
import jax.numpy as jnp
from jax import lax
from jax.experimental import pallas as pl
from jax.experimental.pallas import tpu as pltpu

F32 = jnp.float32
BF16 = jnp.bfloat16

D_MODEL = 1024
MIX_W = 512
N_BRANCH = 3
SB_HEADS = 8
SB_HEAD_DIM = MIX_W // SB_HEADS
RWKV_HEADS = 8
RWKV_HEAD = MIX_W // RWKV_HEADS
DECAY_LORA = 64
VRES_LORA = 32
DECAY_SCALE = math.exp(-0.5)
CA_HEADS = 4
CA_HEAD_DIM = D_MODEL // CA_HEADS
FFN_DIM = 2816
RMS_EPS = 1e-6
GN_EPS = 64e-5

MAIN_COLS = 3 * MIX_W + 3 * MIX_W + N_BRANCH * D_MODEL
LANES = 128
SUBLANES = 8
VMEM_LIMIT = 48 * 1024 * 1024
FFN_VMEM_LIMIT = 56 * 1024 * 1024

SB_BLOCK = 128
SB_QBLOCKS = 8
LOG2E = math.log2(math.e)
SB_DEAD_LOG2 = 104.0 * LOG2E
RWKV_CHUNK = 64
RWKV_PAIR = 2 * RWKV_HEAD
RWKV_CHUNKS_PER_ITER = 4
RWKV_STAGES_PER_UPDATE = 1
MXU_WIDTH = 256


def _dot(a, b):
    return jnp.dot(a.astype(BF16), b.astype(BF16), preferred_element_type=F32)


def _dot_nt(a, b):
    return lax.dot_general(a.astype(BF16), b.astype(BF16), (((1,), (1,)), ((), ())),
                           preferred_element_type=F32)


def _head_sums(x, seg):
    xb = x.astype(BF16)
    n = seg.shape[0]
    return jnp.concatenate([jnp.dot(xb[:, i:i + n], seg, preferred_element_type=F32)
                            for i in range(0, x.shape[1], n)], axis=1)


def _rms(x, g):
    ms = jnp.mean(x * x, axis=-1, keepdims=True)
    return x * lax.rsqrt(ms + RMS_EPS) * g


def _shift_rows(u, prev8, n):
    r = pltpu.roll(u, n, 0)
    rows = lax.broadcasted_iota(jnp.int32, prev8.shape, 0)
    head = jnp.where(rows < n, pltpu.roll(prev8, n, 0), r[:SUBLANES])
    return jnp.concatenate([head, r[SUBLANES:]], axis=0)


def _pick(n, cands):
    for c in cands:
        if n % c == 0:
            return c
    raise ValueError(f"no tile for {n} in {cands}")


def _norm_mm_kernel(x_ref, g_ref, w_ref, o_ref, xn_ref):
    @pl.when(pl.program_id(1) == 0)
    def _():
        xn_ref[...] = _rms(x_ref[...], g_ref[...]).astype(BF16)

    o_ref[...] = jnp.dot(xn_ref[...], w_ref[...], preferred_element_type=F32).astype(o_ref.dtype)


def norm_matmul(x, gain, w, out_dtype):
    m, k = x.shape
    n = w.shape[1]
    tm = _pick(m, (1024, 512, 256, 128))
    tn = _pick(n, (512, 384, 256, 128))
    return pl.pallas_call(
        _norm_mm_kernel,
        grid=(m // tm, n // tn),
        in_specs=[pl.BlockSpec((tm, k), lambda i, j: (i, 0)),
                  pl.BlockSpec((1, k), lambda i, j: (0, 0)),
                  pl.BlockSpec((k, tn), lambda i, j: (0, j))],
        out_specs=pl.BlockSpec((tm, tn), lambda i, j: (i, j)),
        out_shape=jax.ShapeDtypeStruct((m, n), out_dtype),
        scratch_shapes=[pltpu.VMEM((tm, k), BF16)],
        compiler_params=pltpu.CompilerParams(
            dimension_semantics=("parallel", "arbitrary"), vmem_limit_bytes=VMEM_LIMIT),
        name="norm_matmul",
    )(x, gain.reshape(1, k), w)


def _in_proj_kernel(x_ref, g_ref, wm_ref, wr_ref, om_ref, or_ref):
    xn = _rms(x_ref[...], g_ref[...]).astype(BF16)
    for w_ref, o_ref in ((wm_ref, om_ref), (wr_ref, or_ref)):
        cols = w_ref.shape[1]
        for lo in range(0, cols, 2 * MXU_WIDTH):
            n = min(2 * MXU_WIDTH, cols - lo)
            o_ref[:, lo:lo + n] = jnp.dot(xn, w_ref[:, lo:lo + n],
                                          preferred_element_type=F32).astype(o_ref.dtype)


def input_projection(x, gain, w_main, w_rwkv):
    m, k = x.shape
    tm = _pick(m, (512, 256, 128))
    nm, nr = w_main.shape[1], w_rwkv.shape[1]
    gain = gain.reshape(1, k)
    full = lambda a: pl.BlockSpec(a.shape, lambda i: (0,) * a.ndim, pipeline_mode=pl.Buffered(1))
    row = lambda c: pl.BlockSpec((tm, c), lambda i: (i, 0))
    return pl.pallas_call(
        _in_proj_kernel,
        grid=(m // tm,),
        in_specs=[row(k), full(gain), full(w_main), full(w_rwkv)],
        out_specs=(row(nm), row(nr)),
        out_shape=(jax.ShapeDtypeStruct((m, nm), BF16), jax.ShapeDtypeStruct((m, nr), F32)),
        compiler_params=pltpu.CompilerParams(
            dimension_semantics=("parallel",), vmem_limit_bytes=FFN_VMEM_LIMIT),
        name="input_projection",
    )(x, gain, w_main, w_rwkv)


def _stack_pair(x, lane_lo):
    z = jnp.zeros_like(x)
    return jnp.concatenate([jnp.where(lane_lo, x, z), jnp.where(lane_lo, z, x)], axis=0)


def _sb_kernel(q_ref, k_ref, v_ref, tri_ref, o_ref, acc_ref, c_ref, ks_ref, vs_ref):
    tq = SB_BLOCK
    nq = SB_QBLOCKS
    npair = MIX_W // LANES
    zero_blk = ks_ref.shape[0] - npair
    units = [(a, p) for a in range(nq) for p in range(npair)]
    un = range(len(units))
    step = pl.program_id(1)
    lane_lo = lax.broadcasted_iota(jnp.int32, (tq, LANES), 1) < SB_HEAD_DIM
    row = lax.broadcasted_iota(jnp.int32, (tq, 2 * tq), 0)
    col = lax.broadcasted_iota(jnp.int32, (tq, 2 * tq), 1)
    causal = jnp.where(col < tq, col, col - tq) < row
    halves = [slice(0, tq), slice(tq, 2 * tq)]
    tri = tri_ref[...]
    sign = jnp.uint32(0x80000000)

    @pl.when(step == 0)
    def _():
        for p in range(npair):
            ks_ref[zero_blk + p] = jnp.zeros(ks_ref.shape[1:], BF16)
            vs_ref[zero_blk + p] = jnp.zeros(vs_ref.shape[1:], BF16)

    q = []
    for a, p in units:
        rs, ls = slice(a * tq, (a + 1) * tq), slice(p * LANES, (p + 1) * LANES)
        q.append(q_ref[rs, ls] * (SB_HEAD_DIM ** -0.5))
        ks_ref[(step * nq + a) * npair + p] = _stack_pair(k_ref[rs, ls], lane_lo)
        vs_ref[(step * nq + a) * npair + p] = _stack_pair(v_ref[rs, ls], lane_lo)

    def walk(i, diag):
        jb = [step * nq + a - i for a in range(nq)]
        blk = [jnp.where(jb[a] >= 0, jb[a] * npair, zero_blk) + p for a, p in units]
        ks = [ks_ref[blk[u]] for u in un]
        vs = [vs_ref[blk[u]] for u in un]
        z = [_dot_nt(q[u], ks[u]) * LOG2E for u in un]
        nabs = [pltpu.bitcast(pltpu.bitcast(z[u], jnp.uint32) | sign, F32) for u in un]
        sp = [jnp.maximum(z[u], 0.0) + jnp.log(1.0 + jnp.exp2(nabs[u])) * LOG2E for u in un]
        lsig = [z[u] - sp[u] for u in un]
        if diag:
            sp = [jnp.where(causal, sp[u], 0.0) for u in un]
        hi = [sp[u].astype(BF16) for u in un]
        lo = [(sp[u] - hi[u].astype(F32)).astype(BF16) for u in un]
        suf = [jnp.dot(jnp.concatenate([hi[u], lo[u]], axis=1), tri, preferred_element_type=F32)
               for u in un]
        c = [[c_ref[u, e] for e in range(2)] for u in un]
        att = [jnp.concatenate([jnp.exp2(lsig[u][:, hs] - suf[u][:, hs] - c[u][e])
                                for e, hs in enumerate(halves)], axis=1) for u in un]
        if diag:
            att = [jnp.where(causal, att[u], 0.0) for u in un]
        cmin = [None] * nq
        for u, (a, p) in enumerate(units):
            acc_ref[u] += jnp.dot(att[u].astype(BF16), vs[u], preferred_element_type=F32)
            for e, hs in enumerate(halves):
                cn = c[u][e] + jnp.sum(sp[u][:, hs], axis=1, keepdims=True)
                c_ref[u, e] = cn
                cmin[a] = cn if cmin[a] is None else jnp.minimum(cmin[a], cn)
        alive = [jnp.logical_and(jb[a] >= 1, jnp.min(cmin[a]) <= SB_DEAD_LOG2) for a in range(nq)]
        return functools.reduce(jnp.logical_or, alive)

    acc_ref[...] = jnp.zeros_like(acc_ref)
    c_ref[...] = jnp.zeros_like(c_ref)
    alive0 = walk(0, True)
    lax.while_loop(lambda s: s[1], lambda s: (s[0] + 1, walk(s[0], False)), (1, alive0))
    for a in range(nq):
        o_ref[a * tq:(a + 1) * tq, :] = jnp.concatenate(
            [acc_ref[a * npair + p] for p in range(npair)], axis=1).astype(o_ref.dtype)


def stick_breaking(proj_main):
    b, s, _ = proj_main.shape
    tq = SB_BLOCK
    pairs = MIX_W // LANES
    half = jnp.arange(2 * tq) // tq
    pos = jnp.arange(2 * tq)
    same = half[:, None] == half[None, :]
    later = same & (pos[:, None] > pos[None, :])
    tri = jnp.concatenate([later, later], axis=0).astype(BF16)
    nq = SB_QBLOCKS
    blk = lambda c: pl.BlockSpec((None, nq * tq, MIX_W), lambda i, t, c=c: (i, t, c))
    kv_blocks = (s // tq + 1) * pairs
    return pl.pallas_call(
        _sb_kernel,
        grid=(b, s // (nq * tq)),
        in_specs=[blk(3), blk(4), blk(5), pl.BlockSpec(tri.shape, lambda i, t: (0, 0))],
        out_specs=blk(0),
        out_shape=jax.ShapeDtypeStruct((b, s, MIX_W), BF16),
        scratch_shapes=[pltpu.VMEM((nq * pairs, tq, LANES), F32),
                        pltpu.VMEM((nq * pairs, 2, tq, tq), F32),
                        pltpu.VMEM((kv_blocks, 2 * tq, LANES), BF16),
                        pltpu.VMEM((kv_blocks, 2 * tq, LANES), BF16)],
        compiler_params=pltpu.CompilerParams(
            dimension_semantics=("parallel", "arbitrary"), vmem_limit_bytes=FFN_VMEM_LIMIT),
        name="stick_breaking",
    )(proj_main, proj_main, proj_main, tri)


def _rwkv_kernel(*refs, vres):
    if vres:
        (p_ref, mu_ref, wl_ref, al_ref, gl_ref, vl_ref, vec_ref, seg_ref, vfirst_ref,
         y_ref, carry_ref, r_s, lw_s, k_s, v_s, kk_s, b_s, o_s, st_s) = refs
    else:
        (p_ref, mu_ref, wl_ref, al_ref, gl_ref, vec_ref, seg_ref,
         y_ref, vfirst_ref, carry_ref, r_s, lw_s, k_s, v_s, kk_s, b_s, o_s, st_s) = refs
    ts = p_ref.shape[0]
    ck = RWKV_CHUNK
    w_ = MIX_W

    @pl.when(pl.program_id(1) == 0)
    def _():
        carry_ref[...] = jnp.zeros_like(carry_ref)
        st_s[...] = jnp.zeros_like(st_s)

    p = p_ref[...]
    prev = _shift_rows(p, carry_ref[...], 1)
    carry_ref[...] = p[ts - SUBLANES:]
    p = p + mu_ref[...] * (prev - p)
    w0, a0, k_k, k_a, r_k, lnx_w, lnx_b, v0 = (vec_ref[i:i + 1] for i in range(8))
    seg = seg_ref[...]
    r = p[:, :w_]
    k = p[:, w_:2 * w_]
    v = p[:, 2 * w_:3 * w_]
    x_wa = p[:, 3 * w_:3 * w_ + LANES]
    x_g = p[:, 3 * w_ + LANES:3 * w_ + 2 * LANES]
    lw = -DECAY_SCALE * jax.nn.sigmoid(w0 + _dot(jnp.tanh(x_wa), wl_ref[...]))
    a = jax.nn.sigmoid(a0 + _dot(x_wa, al_ref[...]))
    if vres:
        x_v = p[:, 3 * w_ + 2 * LANES:3 * w_ + 3 * LANES]
        v = v + (vfirst_ref[...] - v) * jax.nn.sigmoid(v0 + _dot(x_v, vl_ref[...]))
    else:
        vfirst_ref[...] = v
    g = _dot(jax.nn.sigmoid(x_g), gl_ref[...])
    kk = k * k_k
    kk = kk / jnp.maximum(jnp.sqrt(_head_sums(kk * kk, seg)), 1e-12)
    k = k * (1.0 + (a - 1.0) * k_a)
    r_s[...] = r
    lw_s[...] = lw
    k_s[...] = k
    v_s[...] = v
    kk_s[...] = kk
    b_s[...] = kk * a

    rc = lax.broadcasted_iota(jnp.int32, (ck, ck), 0)
    cc = lax.broadcasted_iota(jnp.int32, (ck, ck), 1)
    tril = jnp.where(cc <= rc, 1.0, 0.0).astype(BF16)
    r2 = lax.broadcasted_iota(jnp.int32, (2 * ck, 2 * ck), 0)
    c2 = lax.broadcasted_iota(jnp.int32, (2 * ck, 2 * ck), 1)
    same = (r2 // ck) == (c2 // ck)
    strict = jnp.logical_and(same, c2 < r2)
    incl = jnp.logical_and(same, c2 <= r2)
    hb = ck // 2
    same_h = (r2 // hb) == (c2 // hb)
    strict_h = jnp.logical_and(same_h, c2 < r2)
    strict_o = jnp.logical_and(strict, jnp.logical_not(same_h))
    rh = lax.broadcasted_iota(jnp.int32, (hb, 2 * ck), 0)
    ch = lax.broadcasted_iota(jnp.int32, (hb, 2 * ck), 1)
    eye_c = jnp.where(ch % hb == rh, 1.0, 0.0)

    def compress(x):
        return functools.reduce(lambda a, b: a + b, [x[i:i + hb] for i in range(0, 2 * ck, hb)])

    def expand(xc):
        return jnp.where(same_h, jnp.concatenate([xc] * (2 * ck // hb), axis=0), 0.0)

    lane_lo = lax.broadcasted_iota(jnp.int32, (ck, RWKV_PAIR), 1) < RWKV_HEAD

    npair = w_ // RWKV_PAIR
    nch = RWKV_CHUNKS_PER_ITER
    sl = [slice(p * RWKV_PAIR, (p + 1) * RWKV_PAIR) for p in range(npair)]

    n2 = 2 * ck
    st = [st_s[p] for p in range(npair)]

    def pre(g, res):
        a_m, b_m, k_m, r_m, v_m, nb_end, k_end, dec, rows = [], [], [], [], [], [], [], [], []
        for j in range(nch):
            rws = slice((g * nch + j) * ck, (g * nch + j + 1) * ck)
            rows.append(rws)
            lw_c = lw_s[rws, :]
            lw_hi = lw_c.astype(BF16)
            lw_lo = (lw_c - lw_hi.astype(F32)).astype(BF16)
            cum = (jnp.dot(tril, lw_hi, preferred_element_type=F32)
                   + jnp.dot(tril, lw_lo, preferred_element_type=F32))
            e_neg = jnp.exp(-cum)
            total = cum[ck - 1:ck]
            e_end = jnp.exp(total - cum)
            e_tot = jnp.exp(total)
            a_all = kk_s[rws, :] * jnp.exp(cum - lw_c)
            b_all = b_s[rws, :]
            k_all = k_s[rws, :]
            r_all = r_s[rws, :] * jnp.exp(cum)
            v_all = v_s[rws, :]
            stk = lambda x: [_stack_pair(x[:, ls], lane_lo).astype(BF16) for ls in sl]
            a_m += stk(a_all)
            b_m += stk(b_all * e_neg)
            k_m += stk(k_all * e_neg)
            r_m += stk(r_all)
            v_m += stk(v_all)
            nb_end += stk(b_all * -e_end)
            k_end += stk(k_all * e_end)
            dec += [jnp.broadcast_to(e_tot[:, ls], (2 * ck, RWKV_PAIR)).T for ls in sl]
        un = range(nch * npair)
        yield
        mm = [_dot_nt(jnp.concatenate([a_m[i], r_m[i]], axis=0),
                      jnp.concatenate([b_m[i], k_m[i]], axis=0)) for i in un]
        yield
        l_ak = [jnp.where(strict, mm[i][:n2, n2:], 0.0) for i in un]
        m_rb = [jnp.where(incl, mm[i][n2:, :n2], 0.0).astype(BF16) for i in un]
        m_rk = [jnp.where(incl, mm[i][n2:, n2:], 0.0).astype(BF16) for i in un]
        dg = [jnp.where(strict_h, mm[i][:n2, :n2], 0.0) for i in un]
        off = [jnp.where(strict_o, mm[i][:n2, :n2], 0.0).astype(BF16) for i in un]
        pc = [compress(dg[i]) for i in un]
        pf = [dg[i].astype(BF16) for i in un]
        tc = [eye_c - pc[i] for i in un]
        yield
        pc = [_dot(pc[i], pf[i]) for i in un]
        yield
        for _i in range(3):
            pf = [expand(pc[i]).astype(BF16) for i in un]
            both = [_dot(jnp.concatenate([pc[i], tc[i]], axis=0), pf[i]) for i in un]
            pc = [both[i][:hb] for i in un]
            tc = [tc[i] + both[i][hb:] for i in un]
            yield
        pf = [expand(pc[i]).astype(BF16) for i in un]
        tc = [tc[i] + _dot(tc[i], pf[i]) for i in un]
        yield
        t_h = [expand(tc[i]) for i in un]
        t_hb = [t_h[i].astype(BF16) for i in un]
        x = [_dot(off[i], t_hb[i]) for i in un]
        yield
        t_inv = [(t_h[i] - _dot(t_hb[i], x[i])).astype(BF16) for i in un]
        yield
        lv = [_dot(l_ak[i], v_m[i]).astype(BF16) for i in un]
        yield
        w12 = [_dot(t_inv[i], jnp.concatenate([a_m[i], lv[i]], axis=1)) for i in un]
        yield
        res.update(
            rows=rows, v_m=v_m, m_rb=m_rb, dec=dec,
            w1=[w12[i][:, :n2].astype(BF16) for i in un], w2=[w12[i][:, n2:] for i in un],
            rk=[jnp.concatenate([r_m[i], m_rk[i]], axis=1) for i in un],
            kbt=[jnp.concatenate([k_end[i].astype(F32).T.astype(BF16),
                                  nb_end[i].astype(F32).T.astype(BF16)], axis=1) for i in un])
        yield

    def seq(res):
        for j in range(nch):
            ids = [j * npair + p for p in range(npair)]
            st_b = [st[p].astype(BF16) for p in range(npair)]
            u = [_dot(res["w1"][i], st_b[p]) + res["w2"][i] for p, i in enumerate(ids)]
            ub = [u[p].astype(BF16) for p in range(npair)]
            yield
            o = [_dot(res["rk"][i], jnp.concatenate([st_b[p], res["v_m"][i]], axis=0))
                 - _dot(res["m_rb"][i], ub[p]) for p, i in enumerate(ids)]
            for p, i in enumerate(ids):
                st[p] = res["dec"][i] * st[p] + _dot(
                    res["kbt"][i], jnp.concatenate([res["v_m"][i], ub[p]], axis=0))
                o_s[res["rows"][j], sl[p]] = o[p][:ck] + o[p][ck:]
            yield

    def drive(main, side, every):
        for n, _ in enumerate(main):
            if side is not None and n % every == every - 1:
                next(side, None)
        for _ in side or ():
            pass

    ngrp = ts // (ck * nch)
    res = [dict() for _ in range(ngrp)]
    drive(pre(0, res[0]), None, 1)
    for grp in range(1, ngrp):
        drive(pre(grp, res[grp]), seq(res[grp - 1]), RWKV_STAGES_PER_UPDATE)
    drive(seq(res[ngrp - 1]), None, 1)
    for p in range(npair):
        st_s[p] = st[p]

    o = o_s[...]
    inv_n = 1.0 / RWKV_HEAD
    mean = _head_sums(o, seg) * inv_n
    d = o - mean
    var = _head_sums(d * d, seg) * inv_n
    gn = d * lax.rsqrt(var + GN_EPS) * lnx_w + lnx_b
    bonus = _head_sums(r * k * r_k, seg) * v
    y_ref[...] = ((gn + bonus) * g).astype(y_ref.dtype)


def rwkv7(p, mu, w_lora, a_lora, g_lora, v_lora, vecs, v_first):
    b, s, cols = p.shape
    vres = v_first is not None
    ts = _pick(s, (512, 256, 128, 64))
    head = jnp.arange(MXU_WIDTH) // RWKV_HEAD
    seg = (head[:, None] == head[None, :]).astype(BF16)
    tile = lambda c: pl.BlockSpec((None, ts, c), lambda i, t: (i, t, 0))
    full = lambda a: pl.BlockSpec(a.shape, lambda i, t: (0,) * a.ndim)
    ins = [p, mu, w_lora, a_lora, g_lora] + ([v_lora] if vres else []) + [vecs, seg]
    in_specs = [tile(cols)] + [full(a) for a in ins[1:]]
    y_shape = jax.ShapeDtypeStruct((b, s, MIX_W), BF16)
    if vres:
        ins.append(v_first)
        in_specs.append(tile(MIX_W))
        out_shape, out_specs = y_shape, tile(MIX_W)
    else:
        out_shape = (y_shape, jax.ShapeDtypeStruct((b, s, MIX_W), F32))
        out_specs = (tile(MIX_W), tile(MIX_W))
    pairs = MIX_W // RWKV_PAIR
    scratch = ([pltpu.VMEM((SUBLANES, cols), F32)] + [pltpu.VMEM((ts, MIX_W), F32)] * 7
               + [pltpu.VMEM((pairs, RWKV_PAIR, RWKV_PAIR), F32)])
    out = pl.pallas_call(
        functools.partial(_rwkv_kernel, vres=vres),
        grid=(b, s // ts),
        in_specs=in_specs, out_specs=out_specs, out_shape=out_shape,
        scratch_shapes=scratch,
        compiler_params=pltpu.CompilerParams(
            dimension_semantics=("parallel", "arbitrary"), vmem_limit_bytes=VMEM_LIMIT),
        name="rwkv7",
    )(*ins)
    return (out, v_first) if vres else out


def _mix_ca_kernel(cb_ref, cc_ref, cx_ref, cw_ref, ys_ref, yr_ref, gate_ref, gb_ref, bp_ref, wmo_ref,
                   h_ref, g_ref, wq_ref, k_ref, v_ref, wo_ref, o_ref, carry_ref):
    @pl.when(pl.program_id(1) == 0)
    def _():
        carry_ref[...] = jnp.zeros_like(carry_ref)

    u = cc_ref[...].astype(F32) * cx_ref[...].astype(F32)
    prev8 = carry_ref[...]
    conv = (cw_ref[2:3] * u + cw_ref[1:2] * _shift_rows(u, prev8, 1)
            + cw_ref[0:1] * _shift_rows(u, prev8, 2))
    carry_ref[...] = u[u.shape[0] - SUBLANES:]
    y_conv = (cb_ref[...].astype(F32) * conv).astype(BF16)

    merged = None
    for n, y in enumerate((y_conv, ys_ref[...], yr_ref[...])):
        branch = jnp.dot(y, bp_ref[n], preferred_element_type=F32)
        gate = jax.nn.sigmoid(gate_ref[:, n * D_MODEL:(n + 1) * D_MODEL].astype(F32) + gb_ref[n:n + 1])
        merged = gate * branch if merged is None else merged + gate * branch
    h = h_ref[...] + _dot(merged, wmo_ref[...])

    q = _dot(_rms(h, g_ref[...]), wq_ref[...]).astype(BF16)
    heads = [slice(hd * CA_HEAD_DIM, (hd + 1) * CA_HEAD_DIM) for hd in range(CA_HEADS)]
    sc = [_dot_nt(q[:, sl], k_ref[:, sl]) * (CA_HEAD_DIM ** -0.5) for sl in heads]
    e = [jnp.exp(x - jnp.max(x, axis=-1, keepdims=True)) for x in sc]
    probs = [x / jnp.sum(x, axis=-1, keepdims=True) for x in e]
    outs = [_dot(pr, v_ref[:, sl]) for pr, sl in zip(probs, heads)]
    o_ref[...] = h + _dot(jnp.concatenate(outs, axis=1), wo_ref[...])


def mix_cross_attention(h, proj_main, y_sb, y_rwkv, conv_w, gate_b, branch_proj, w_mix_out,
                        kv, gain_ca, wq, wo):
    b, s, d = h.shape
    mem = kv.shape[1]
    tm = _pick(s, (512, 256, 128))
    full = lambda a: pl.BlockSpec(a.shape, lambda i, t: (0,) * a.ndim, pipeline_mode=pl.Buffered(1))
    tile = lambda c, j=0: pl.BlockSpec((None, tm, c), lambda i, t, j=j: (i, t, j))
    return pl.pallas_call(
        _mix_ca_kernel,
        grid=(b, s // tm),
        in_specs=[tile(MIX_W, 0), tile(MIX_W, 1), tile(MIX_W, 2), full(conv_w),
                  tile(MIX_W), tile(MIX_W), tile(N_BRANCH * D_MODEL, 1),
                  full(gate_b), full(branch_proj), full(w_mix_out),
                  tile(d), full(gain_ca), full(wq),
                  pl.BlockSpec((None, mem, d), lambda i, t: (i, 0, 0)),
                  pl.BlockSpec((None, mem, d), lambda i, t: (i, 0, 1)),
                  full(wo)],
        out_specs=tile(d),
        out_shape=jax.ShapeDtypeStruct((b, s, d), F32),
        scratch_shapes=[pltpu.VMEM((SUBLANES, MIX_W), F32)],
        compiler_params=pltpu.CompilerParams(
            dimension_semantics=("parallel", "arbitrary"), vmem_limit_bytes=VMEM_LIMIT),
        name="mix_cross_attention",
    )(proj_main, proj_main, proj_main, conv_w, y_sb, y_rwkv, proj_main, gate_b, branch_proj,
      w_mix_out, h, gain_ca, wq, kv, kv, wo)


def _conv_ffn_kernel(h_ref, g_ref, wup_ref, cw_ref, cb_ref, wdn_ref, gf_ref, o_ref, carry_ref,
                     act_ref, *, fc, final_norm):
    tm = h_ref.shape[0]
    nc = FFN_DIM // fc

    @pl.when(pl.program_id(1) == 0)
    def _():
        carry_ref[...] = jnp.zeros_like(carry_ref)

    h = h_ref[...]
    hn = _rms(h, g_ref[...]).astype(BF16)
    for c in range(nc):
        halves = []
        for part in range(2):
            lo = part * FFN_DIM + c * fc
            u = jnp.dot(hn, wup_ref[:, lo:lo + fc], preferred_element_type=F32)
            prev8 = carry_ref[part * nc + c]
            cw = cw_ref[:, lo:lo + fc]
            halves.append(cw[2:3] * u + cw[1:2] * _shift_rows(u, prev8, 1)
                          + cw[0:1] * _shift_rows(u, prev8, 2) + cb_ref[:, lo:lo + fc])
            carry_ref[part * nc + c] = u[tm - SUBLANES:]
        act_ref[:, c * fc:(c + 1) * fc] = (jax.nn.silu(halves[0]) * halves[1]).astype(BF16)
    out = h + jnp.dot(act_ref[...], wdn_ref[...], preferred_element_type=F32)
    o_ref[...] = _rms(out, gf_ref[...]) if final_norm else out


def conv_ffn(h, gain, w_up, conv_w, conv_b, w_down, gain_final, final_norm):
    b, s, d = h.shape
    tm = _pick(s, (512, 256, 128))
    fc = 256
    full = lambda a: pl.BlockSpec(a.shape, lambda i, t: (0,) * a.ndim, pipeline_mode=pl.Buffered(1))
    return pl.pallas_call(
        functools.partial(_conv_ffn_kernel, fc=fc, final_norm=final_norm),
        grid=(b, s // tm),
        in_specs=[pl.BlockSpec((None, tm, d), lambda i, t: (i, t, 0)),
                  full(gain), full(w_up), full(conv_w), full(conv_b), full(w_down), full(gain_final)],
        out_specs=pl.BlockSpec((None, tm, d), lambda i, t: (i, t, 0)),
        out_shape=jax.ShapeDtypeStruct((b, s, d), F32),
        scratch_shapes=[pltpu.VMEM((2 * FFN_DIM // fc, SUBLANES, fc), F32),
                        pltpu.VMEM((tm, FFN_DIM), BF16)],
        compiler_params=pltpu.CompilerParams(
            dimension_semantics=("parallel", "arbitrary"), vmem_limit_bytes=FFN_VMEM_LIMIT),
        name="conv_ffn",
    )(h, gain, w_up, conv_w, conv_b, w_down, gain_final)


def _pad_rows(w, rows):
    return jnp.concatenate([w, jnp.zeros((rows - w.shape[0],) + w.shape[1:], w.dtype)], axis=0)


def kernel(x, mem, norm_mix, w_comb, conv_w, mu_rwkv, w0, w_lora, a0, a_lora, g_lora, k_k, k_a, r_k, lnx_w, lnx_b, w_vres, mu_vres, v0, v_lora, branch_proj, gate_b, w_mix_out, norm_ca, norm_mem, ca_wq, ca_wkv, ca_wo, norm_ffn, ffn_up, ffn_conv_w, ffn_conv_b, ffn_down, norm_final):
    b, s, d = x.shape
    depth = w_comb.shape[0]
    mem_len = mem.shape[1]
    h = x
    v_first = None
    zero_lora = jnp.zeros((DECAY_LORA, MIX_W), F32)
    for l in range(depth):
        h2 = h.reshape(b * s, d)
        w_main = w_comb[l, :, :MAIN_COLS].astype(BF16)
        w_rwkv = w_comb[l, :, MAIN_COLS:]
        mu = mu_rwkv[l]
        if l > 0:
            pad = LANES - VRES_LORA
            w_rwkv = jnp.concatenate([w_rwkv, w_vres[l - 1], jnp.zeros((d, pad), F32)], axis=1)
            mu = jnp.concatenate([mu, mu_vres[l - 1], jnp.zeros((pad,), F32)])
        proj_main, proj_rwkv = input_projection(h2, norm_mix[l], w_main, w_rwkv.astype(BF16))
        proj_main = proj_main.reshape(b, s, MAIN_COLS)
        proj_rwkv = proj_rwkv.reshape(b, s, -1)

        y_sb = stick_breaking(proj_main)
        vecs = jnp.stack([w0[l], a0[l], k_k[l], k_a[l], r_k[l].reshape(-1), lnx_w[l], lnx_b[l],
                          v0[l - 1] if l > 0 else jnp.zeros((MIX_W,), F32)])
        y_rwkv, v_first = rwkv7(
            proj_rwkv, mu.reshape(1, -1),
            jnp.concatenate([w_lora[l], zero_lora], axis=0).astype(BF16),
            jnp.concatenate([zero_lora, a_lora[l]], axis=0).astype(BF16),
            g_lora[l].astype(BF16),
            _pad_rows(v_lora[l - 1], LANES).astype(BF16) if l > 0 else None,
            vecs, v_first)

        kv = norm_matmul(mem.reshape(b * mem_len, d), norm_mem[l], ca_wkv[l].astype(BF16), BF16)
        h = mix_cross_attention(h, proj_main, y_sb, y_rwkv, conv_w[l], gate_b[l],
                                branch_proj[l].astype(BF16), w_mix_out[l].astype(BF16),
                                kv.reshape(b, mem_len, 2 * d), norm_ca[l].reshape(1, d),
                                ca_wq[l].astype(BF16), ca_wo[l].astype(BF16))

        h = conv_ffn(h, norm_ffn[l].reshape(1, d), ffn_up[l].astype(BF16), ffn_conv_w[l],
                     ffn_conv_b[l].reshape(1, -1), ffn_down[l].astype(BF16),
                     norm_final.reshape(1, d), final_norm=(l == depth - 1))
    return h
```
